```python
import math
import jax
import jax.numpy as jnp
from jax import lax
import numpy as np

D_MODEL = 1024
BATCH = 8
SEQ = 4096
DEPTH = 2

N_MIXERS = 2
N_A_LAYERS = (DEPTH + 1) // 2
N_B_LAYERS = DEPTH // 2
A_HEADS = 16
A_HEAD_DIM = 64
A_V_DIM = 64
A_Q_RANK = 256
A_KV_RANK = 128
IDX_HEADS = 8
IDX_DIM = 64
IDX_TOPK_MAX = 256
IDX_TOPK_FRAC = 4
A_IN_WIDTH = A_Q_RANK + A_KV_RANK + IDX_DIM + IDX_HEADS
B_HEADS = 16
B_HEAD_DIM = 64
B_IN_WIDTH = 3 * B_HEADS * B_HEAD_DIM + B_HEADS
REL_BUCKETS = 32
REL_MAX_DIST = 128
N_EXPERTS = 32
TOP_K = 4
D_EXPERT = 1024
SWIGLU_ALPHA = 1.702
SWIGLU_LIMIT = 7.0
EXPERT_BLOCK = 512
Q_BLOCK = 128
LN_EPS = 1e-5
RMS_EPS = 1e-6
DEEPNORM_ALPHA = (2.0 * DEPTH) ** 0.25
DEEPNORM_BETA = (8.0 * DEPTH) ** -0.25

kernel_name = 'hybrid_dsa_fox_moe_deepnorm'


def layer_norm(x, g, b):
    xf = x.astype(jnp.float32)
    mu = jnp.mean(xf, -1, keepdims=True)
    var = jnp.mean(jnp.square(xf - mu), -1, keepdims=True)
    return ((xf - mu) * lax.rsqrt(var + LN_EPS) * g + b).astype(x.dtype)


def rms_norm(x, g):
    xf = x.astype(jnp.float32)
    y = xf * lax.rsqrt(jnp.mean(xf * xf, -1, keepdims=True) + RMS_EPS)
    return (y * g).astype(x.dtype)


def t5_bucket(dist):
    max_exact = REL_BUCKETS // 2
    n = jnp.maximum(dist, 0)
    large = max_exact + (jnp.log(jnp.maximum(n, 1).astype(jnp.float32) / max_exact)
                         / math.log(REL_MAX_DIST / max_exact)
                         * (REL_BUCKETS - max_exact)).astype(jnp.int32)
    large = jnp.minimum(large, REL_BUCKETS - 1)
    return jnp.where(n < max_exact, n, large)


def dsa_mixer(x, w_in, g_q, g_kv, g_kidx, b_kidx, w_uq, w_uk, w_uv, w_qidx, w_o, rel_table):
    bsz, seq, _ = x.shape
    topk = min(IDX_TOPK_MAX, seq // IDX_TOPK_FRAC)
    scale = A_HEAD_DIM ** -0.5
    proj = x @ w_in
    c_q, c_kv, k_idx, w_idx = jnp.split(
        proj, [A_Q_RANK, A_Q_RANK + A_KV_RANK, A_Q_RANK + A_KV_RANK + IDX_DIM], axis=-1)
    c_q = rms_norm(c_q, g_q)
    c_kv = rms_norm(c_kv, g_kv)
    k_idx = layer_norm(k_idx, g_kidx, b_kidx)
    q = jnp.einsum('bsr,rhd->bshd', c_q, w_uq)
    q_lat = jnp.einsum('bshd,rhd->bshr', q, w_uk) * scale
    q_idx = jnp.einsum('bsr,rhd->bshd', c_q, w_qidx)
    w_idx = w_idx * (IDX_HEADS ** -0.5 * IDX_DIM ** -0.5)
    key_pos = jnp.arange(seq)

    def block(i):
        t0 = i * Q_BLOCK
        qi = lax.dynamic_slice_in_dim(q_idx, t0, Q_BLOCK, axis=1)
        wi = lax.dynamic_slice_in_dim(w_idx, t0, Q_BLOCK, axis=1)
        ql = lax.dynamic_slice_in_dim(q_lat, t0, Q_BLOCK, axis=1)
        q_pos = t0 + jnp.arange(Q_BLOCK)
        causal = key_pos[None, :] <= q_pos[:, None]
        rel = jax.nn.relu(jnp.einsum('bthd,bsd->bths', qi, k_idx,
                                     preferred_element_type=jnp.float32))
        score = jnp.einsum('bths,bth->bts', rel, wi.astype(jnp.float32))
        score = jnp.where(causal[None], score, -jnp.inf)
        _, sel = lax.top_k(score, topk)
        c_sel = jax.vmap(lambda c, ix: c[ix])(c_kv, sel)
        valid = sel <= q_pos[None, :, None]
        bias = rel_table[t5_bucket(q_pos[None, :, None] - sel)]
        logits = jnp.einsum('bthr,btkr->bthk', ql, c_sel, preferred_element_type=jnp.float32)
        logits = logits + jnp.moveaxis(bias, -1, 2).astype(jnp.float32)
        logits = jnp.where(valid[:, :, None, :], logits, -jnp.inf)
        p = jax.nn.softmax(logits, axis=-1).astype(x.dtype)
        o_lat = jnp.einsum('bthk,btkr->bthr', p, c_sel)
        return jnp.einsum('bthr,rhd->bthd', o_lat, w_uv)

    out = lax.map(block, jnp.arange(seq // Q_BLOCK))
    out = out.transpose(1, 0, 2, 3, 4).reshape(bsz, seq, A_HEADS * A_V_DIM)
    return out @ w_o


def fox_mixer(x, w_in, b_f, w_o):
    bsz, seq, _ = x.shape
    hd = B_HEADS * B_HEAD_DIM
    scale = B_HEAD_DIM ** -0.5
    proj = x @ w_in
    q = proj[..., :hd].reshape(bsz, seq, B_HEADS, B_HEAD_DIM)
    k = proj[..., hd:2 * hd].reshape(bsz, seq, B_HEADS, B_HEAD_DIM)
    v = proj[..., 2 * hd:3 * hd].reshape(bsz, seq, B_HEADS, B_HEAD_DIM)
    log_f = jax.nn.log_sigmoid(proj[..., 3 * hd:].astype(jnp.float32) + b_f.astype(jnp.float32))
    cum = jnp.cumsum(log_f, axis=1).transpose(0, 2, 1)
    key_pos = jnp.arange(seq)

    def block(i):
        t0 = i * Q_BLOCK
        qi = lax.dynamic_slice_in_dim(q, t0, Q_BLOCK, axis=1)
        ci = lax.dynamic_slice_in_dim(cum, t0, Q_BLOCK, axis=2)
        q_pos = t0 + jnp.arange(Q_BLOCK)
        logits = jnp.einsum('bthd,bshd->bhts', qi, k, preferred_element_type=jnp.float32) * scale
        logits = logits + ci[:, :, :, None] - cum[:, :, None, :]
        logits = jnp.where((key_pos[None, :] <= q_pos[:, None])[None, None], logits, -jnp.inf)
        p = jax.nn.softmax(logits, axis=-1).astype(x.dtype)
        return jnp.einsum('bhts,bshd->bthd', p, v)

    out = lax.map(block, jnp.arange(seq // Q_BLOCK))
    out = out.transpose(1, 0, 2, 3, 4).reshape(bsz, seq, hd)
    return out @ w_o


def moe(x, w_router, b_router, w_gu, b_gu, w_down, b_down):
    bsz, seq, dm = x.shape
    xt = x.reshape(-1, dm)
    n_tok = xt.shape[0]
    n_pairs = n_tok * TOP_K
    logits = (xt @ w_router + b_router).astype(jnp.float32)
    top_v, top_e = lax.top_k(logits, TOP_K)
    gates = jax.nn.softmax(top_v, axis=-1)
    flat_e = top_e.reshape(-1)
    flat_tok = (jnp.arange(n_pairs, dtype=jnp.int32) // TOP_K)
    flat_g = gates.reshape(-1)
    order = jnp.argsort(flat_e)
    e_sorted = flat_e[order]
    counts = jnp.bincount(flat_e, length=N_EXPERTS)
    starts = jnp.cumsum(counts) - counts
    padded = (counts + EXPERT_BLOCK - 1) // EXPERT_BLOCK * EXPERT_BLOCK
    pad_ends = jnp.cumsum(padded)
    pad_starts = pad_ends - padded
    dest = pad_starts[e_sorted] + (jnp.arange(n_pairs) - starts[e_sorted])
    n_slots = -(-n_pairs // EXPERT_BLOCK) * EXPERT_BLOCK + N_EXPERTS * EXPERT_BLOCK
    slot_tok = jnp.full((n_slots,), n_tok, jnp.int32).at[dest].set(flat_tok[order])
    slot_gate = jnp.zeros((n_slots,), x.dtype).at[dest].set(flat_g[order].astype(x.dtype))
    n_chunks = n_slots // EXPERT_BLOCK
    chunk_start = jnp.arange(n_chunks) * EXPERT_BLOCK
    chunk_e = jnp.minimum(jnp.searchsorted(pad_ends, chunk_start, side='right'), N_EXPERTS - 1)
    xt_pad = jnp.concatenate([xt, jnp.zeros((1, dm), xt.dtype)], axis=0)

    def expert_block(args):
        tok, e = args
        h = xt_pad[tok] @ w_gu[e] + b_gu[e]
        gate, up = h[:, :D_EXPERT], h[:, D_EXPERT:]
        gate = jnp.minimum(gate, SWIGLU_LIMIT)
        up = jnp.clip(up, -SWIGLU_LIMIT, SWIGLU_LIMIT)
        h = (up + 1.0) * (gate * jax.nn.sigmoid(SWIGLU_ALPHA * gate))
        return h @ w_down[e] + b_down[e]

    y = lax.map(expert_block, (slot_tok.reshape(n_chunks, EXPERT_BLOCK), chunk_e))
    y = y.reshape(n_slots, dm) * slot_gate[:, None]
    out = jax.ops.segment_sum(y, slot_tok, num_segments=n_tok + 1)[:n_tok]
    return out.reshape(bsz, seq, dm)


def _normal(k, shape, scale):
    return jax.random.normal(k, shape, jnp.float32) * scale


def setup_inputs(seed: int = 0) -> dict:
    key = jax.random.key(seed)
    ks = jax.random.split(key, 32)
    D = D_MODEL
    hd = B_HEADS * B_HEAD_DIM
    b_qkv = _normal(ks[12], (N_B_LAYERS, D, 3 * hd), D ** -0.5)
    b_f_cols = _normal(ks[13], (N_B_LAYERS, D, B_HEADS), 0.1 * D ** -0.5)
    return {
        'x': _normal(ks[0], (BATCH, SEQ, D), 1.0),
        'rel_table': _normal(ks[1], (REL_BUCKETS, A_HEADS), 0.5),
        'a_w_in': _normal(ks[2], (N_A_LAYERS, D, A_IN_WIDTH), D ** -0.5),
        'a_g_q': 1.0 + _normal(ks[3], (N_A_LAYERS, A_Q_RANK), 0.02),
        'a_g_kv': 1.0 + _normal(ks[4], (N_A_LAYERS, A_KV_RANK), 0.02),
        'a_g_kidx': 1.0 + _normal(ks[5], (N_A_LAYERS, IDX_DIM), 0.02),
        'a_b_kidx': _normal(ks[6], (N_A_LAYERS, IDX_DIM), 0.02),
        'a_w_uq': _normal(ks[7], (N_A_LAYERS, A_Q_RANK, A_HEADS, A_HEAD_DIM), A_Q_RANK ** -0.5),
        'a_w_uk': _normal(ks[8], (N_A_LAYERS, A_KV_RANK, A_HEADS, A_HEAD_DIM), A_KV_RANK ** -0.5),
        'a_w_uv': _normal(ks[9], (N_A_LAYERS, A_KV_RANK, A_HEADS, A_V_DIM), A_KV_RANK ** -0.5),
        'a_w_qidx': _normal(ks[10], (N_A_LAYERS, A_Q_RANK, IDX_HEADS, IDX_DIM), A_Q_RANK ** -0.5),
        'a_w_o': _normal(ks[11], (N_A_LAYERS, A_HEADS * A_V_DIM, D), DEEPNORM_BETA * (A_HEADS * A_V_DIM) ** -0.5),
        'b_w_in': jnp.concatenate([b_qkv, b_f_cols], axis=-1),
        'b_b_f': jax.random.uniform(ks[14], (N_B_LAYERS, B_HEADS), jnp.float32, 1.0, 4.0),
        'b_w_o': _normal(ks[15], (N_B_LAYERS, hd, D), DEEPNORM_BETA * hd ** -0.5),
        'ln_mix_g': 1.0 + _normal(ks[16], (DEPTH, D), 0.02),
        'ln_mix_b': _normal(ks[17], (DEPTH, D), 0.02),
        'ln_ffn_g': 1.0 + _normal(ks[18], (DEPTH, D), 0.02),
        'ln_ffn_b': _normal(ks[19], (DEPTH, D), 0.02),
        'w_router': _normal(ks[20], (DEPTH, D, N_EXPERTS), D ** -0.5),
        'b_router': _normal(ks[21], (DEPTH, N_EXPERTS), 0.01),
        'w_gu': _normal(ks[22], (DEPTH, N_EXPERTS, D, 2 * D_EXPERT), D ** -0.5),
        'b_gu': _normal(ks[23], (DEPTH, N_EXPERTS, 2 * D_EXPERT), 0.02),
        'w_down': _normal(ks[24], (DEPTH, N_EXPERTS, D_EXPERT, D), DEEPNORM_BETA * D_EXPERT ** -0.5),
        'b_down': _normal(ks[25], (DEPTH, N_EXPERTS, D), 0.02),
    }


def reference(x, rel_table, a_w_in, a_g_q, a_g_kv, a_g_kidx, a_b_kidx, a_w_uq, a_w_uk, a_w_uv,
              a_w_qidx, a_w_o, b_w_in, b_b_f, b_w_o, ln_mix_g, ln_mix_b, ln_ffn_g, ln_ffn_b,
              w_router, b_router, w_gu, b_gu, w_down, b_down):
    for i in range(DEPTH):
        j = i // N_MIXERS
        if i % N_MIXERS == 0:
            h = dsa_mixer(x, a_w_in[j], a_g_q[j], a_g_kv[j], a_g_kidx[j], a_b_kidx[j],
                          a_w_uq[j], a_w_uk[j], a_w_uv[j], a_w_qidx[j], a_w_o[j], rel_table)
        else:
            h = fox_mixer(x, b_w_in[j], b_b_f[j], b_w_o[j])
        x = layer_norm(DEEPNORM_ALPHA * x + h, ln_mix_g[i], ln_mix_b[i])
        f = moe(x, w_router[i], b_router[i], w_gu[i], b_gu[i], w_down[i], b_down[i])
        x = layer_norm(DEEPNORM_ALPHA * x + f, ln_ffn_g[i], ln_ffn_b[i])
    return x
```

```python
import functools
import math

import numpy as np
import jax
import jax.numpy as jnp
from jax import lax
from jax.experimental import pallas as pl
from jax.experimental.pallas import tpu as pltpu

F32 = jnp.float32
BF16 = jnp.bfloat16
I32 = jnp.int32

A_HEADS = 16
A_HEAD_DIM = 64
A_V_DIM = 64
A_Q_RANK = 256
A_KV_RANK = 128
IDX_HEADS = 8
IDX_DIM = 64
IDX_TOPK_MAX = 256
IDX_TOPK_FRAC = 4
B_HEADS = 16
B_HEAD_DIM = 64
REL_BUCKETS = 32
REL_MAX_DIST = 128
TOP_K = 4
SWIGLU_ALPHA = 1.702
SWIGLU_LIMIT = 7.0
LN_EPS = 1e-5
RMS_EPS = 1e-6
DEPTH = 2
DEEPNORM_ALPHA = (2.0 * DEPTH) ** 0.25

LANES = 128
NEG = -1e30
INT_MIN = -2147483648
MASK_KEY = INT_MIN + 1
VMEM_LIMIT = 56 * 1024 * 1024

TOK_TILE = 512
DSA_T = 128
DSA_KC = 256
FOX_T = 256
MOE_TM = 512
ROW_TILE = 256


def _dot(a, b):
    return jnp.dot(a, b, preferred_element_type=F32)


def _dot_t(a, b):
    return lax.dot_general(a, b, (((1,), (1,)), ((), ())), preferred_element_type=F32)


def _cparams(sem):
    return pltpu.CompilerParams(dimension_semantics=sem, vmem_limit_bytes=VMEM_LIMIT)


def _layer_norm(v, g, b):
    mu = jnp.mean(v, axis=-1, keepdims=True)
    c = v - mu
    var = jnp.mean(c * c, axis=-1, keepdims=True)
    return c * lax.rsqrt(var + LN_EPS) * g + b


def _rms_norm(v, g):
    return v * lax.rsqrt(jnp.mean(v * v, axis=-1, keepdims=True) + RMS_EPS) * g


def _dsa_proj_kernel(x_ref, wcq_ref, wckv_ref, wkw_ref, gq_ref, gkv_ref, gki_ref, bki_ref,
                     wuq_ref, wuk_ref, wqi_ref,
                     ckv_ref, kidx_ref, widx_ref, qlat_ref, qidx_ref):
    xb = x_ref[...].astype(BF16)
    cq = _rms_norm(_dot(xb, wcq_ref[...]), gq_ref[...])
    ckv = _rms_norm(_dot(xb, wckv_ref[...]), gkv_ref[...])
    ckv_ref[...] = ckv.astype(BF16)
    kw = _dot(xb, wkw_ref[...])
    kidx = _layer_norm(kw[:, :IDX_DIM], gki_ref[...], bki_ref[...])
    kidx_ref[...] = kidx.astype(BF16)
    widx_ref[...] = kw[:, IDX_DIM:IDX_DIM + IDX_HEADS] * (IDX_HEADS ** -0.5 * IDX_DIM ** -0.5)
    cqb = cq.astype(BF16)
    qb = _dot(cqb, wuq_ref[...]).astype(BF16)
    scale = A_HEAD_DIM ** -0.5
    for p in range(A_HEADS // 2):
        ql = _dot(qb[:, p * LANES:(p + 1) * LANES], wuk_ref[p]) * scale
        qlat_ref[2 * p] = ql[:, :A_KV_RANK].astype(BF16)
        qlat_ref[2 * p + 1] = ql[:, A_KV_RANK:].astype(BF16)
    for h in range(IDX_HEADS):
        qidx_ref[h] = _dot(cqb, wqi_ref[h]).astype(BF16)


def _dsa_proj(x2, wcq, wckv, wkw, gq, gkv, gki, bki, wuq, wukbd, wqi):
    n, d = x2.shape
    tn = min(TOK_TILE, n)
    full = lambda a: pl.BlockSpec(a.shape, lambda i, _nd=a.ndim: (0,) * _nd)
    return pl.pallas_call(
        _dsa_proj_kernel,
        grid=(n // tn,),
        in_specs=[pl.BlockSpec((tn, d), lambda i: (i, 0))] + [full(a) for a in (wcq, wckv, wkw, gq, gkv, gki, bki, wuq, wukbd, wqi)],
        out_specs=[
            pl.BlockSpec((tn, A_KV_RANK), lambda i: (i, 0)),
            pl.BlockSpec((tn, IDX_DIM), lambda i: (i, 0)),
            pl.BlockSpec((tn, IDX_HEADS), lambda i: (i, 0)),
            pl.BlockSpec((A_HEADS, tn, A_KV_RANK), lambda i: (0, i, 0)),
            pl.BlockSpec((IDX_HEADS, tn, IDX_DIM), lambda i: (0, i, 0)),
        ],
        out_shape=[
            jax.ShapeDtypeStruct((n, A_KV_RANK), BF16),
            jax.ShapeDtypeStruct((n, IDX_DIM), BF16),
            jax.ShapeDtypeStruct((n, IDX_HEADS), F32),
            jax.ShapeDtypeStruct((A_HEADS, n, A_KV_RANK), BF16),
            jax.ShapeDtypeStruct((IDX_HEADS, n, IDX_DIM), BF16),
        ],
        compiler_params=_cparams(("parallel",)),
        name="dsa_proj",
    )(x2, wcq, wckv, wkw, gq, gkv, gki, bki, wuq, wukbd, wqi)


def _dsa_attn_kernel(topk, qidx_ref, widx_ref, kidx_ref, ckv_ref, qlat_ref, bias_ref, wuv_ref, o_ref,
                     keys_scr, jstar_scr, m_scr, l_scr, acc_scr):
    T, KC, H = DSA_T, DSA_KC, A_HEADS
    i = pl.program_id(1)
    t0 = i * T
    n_blk = i + 1
    n_sc = (n_blk * T + KC - 1) // KC
    t_abs = t0 + lax.broadcasted_iota(I32, (T, 1), 0)
    w = widx_ref[...]

    def score_chunk(c, carry):
        k0 = pl.multiple_of(c * KC, KC)
        kc = kidx_ref[pl.ds(k0, KC), :]
        acc = jnp.zeros((T, KC), F32)
        for h in range(IDX_HEADS):
            acc = acc + w[:, h:h + 1] * jnp.maximum(_dot_t(qidx_ref[h], kc), 0.0)
        acc = jnp.where(acc == 0.0, 0.0, acc)
        bits = pltpu.bitcast(acc, I32)
        key = bits ^ ((bits >> 31) & 0x7FFFFFFF)
        s_abs = k0 + lax.broadcasted_iota(I32, (1, KC), 1)
        key = jnp.where(s_abs <= t_abs, key, MASK_KEY)
        for g in range(KC // LANES):
            keys_scr[c * (KC // LANES) + g] = key[:, g * LANES:(g + 1) * LANES]
        return carry

    lax.fori_loop(0, n_sc, score_chunk, 0)

    def count_ge(cand_s):
        cand_b = jnp.broadcast_to(cand_s, (T, LANES))

        def body(j, part):
            return part + jnp.where(keys_scr[j] >= cand_b, 1.0, 0.0)

        part = lax.fori_loop(0, n_blk, body, jnp.zeros((T, LANES), F32))
        return jnp.sum(part, axis=1, keepdims=True)

    def bit_body(bi, ans):
        cand = ans | jnp.left_shift(jnp.int32(1), 31 - bi)
        cnt = count_ge(cand ^ INT_MIN)
        return jnp.where(cnt >= topk, cand, ans)

    ans = lax.fori_loop(0, 32, bit_body, jnp.zeros((T, 1), I32))
    thr = ans ^ INT_MIN
    thr_b = jnp.broadcast_to(thr, (T, LANES))

    def count_gt_eq():
        def body(j, parts):
            kk = keys_scr[j]
            return (parts[0] + jnp.where(kk > thr_b, 1.0, 0.0), parts[1] + jnp.where(kk == thr_b, 1.0, 0.0))

        z = jnp.zeros((T, LANES), F32)
        pg, pe = lax.fori_loop(0, n_blk, body, (z, z))
        return jnp.sum(pg, axis=1, keepdims=True), jnp.sum(pe, axis=1, keepdims=True)

    cnt_gt, cnt_eq = count_gt_eq()
    need = topk - cnt_gt
    s_total = keys_scr.shape[0] * LANES
    jstar_scr[...] = jnp.full((T, LANES), s_total, I32)
    tie = jnp.max(jnp.where(cnt_gt + cnt_eq > topk, 1.0, 0.0)) > 0.0

    @pl.when(tie)
    def _():
        n_bits = max(1, (s_total - 1).bit_length())

        def jbit(bi, lo):
            cand = lo + jnp.left_shift(jnp.int32(1), n_bits - 1 - bi)
            cand_b = jnp.broadcast_to(cand, (T, LANES))

            def body(j, part):
                pos = j * LANES + lax.broadcasted_iota(I32, (T, LANES), 1)
                hit = jnp.where(keys_scr[j] == thr_b, 1.0, 0.0)
                return part + jnp.where(pos < cand_b, hit, 0.0)

            part = lax.fori_loop(0, n_blk, body, jnp.zeros((T, LANES), F32))
            c = jnp.sum(part, axis=1, keepdims=True)
            return jnp.where(c < need, cand, lo)

        lo = lax.fori_loop(0, n_bits, jbit, jnp.zeros((T, 1), I32))
        jstar_scr[...] = jnp.broadcast_to(lo, (T, LANES))

    jstar = jstar_scr[:, 0:1]

    m_scr[...] = jnp.full(m_scr.shape, NEG, F32)
    l_scr[...] = jnp.zeros(l_scr.shape, F32)
    acc_scr[...] = jnp.zeros(acc_scr.shape, F32)
    q = qlat_ref[...].reshape(H * T, A_KV_RANK)

    def attend(blk0, k0, cond_fn, bias):
        kv = ckv_ref[pl.ds(k0, KC), :]
        s = _dot_t(q, kv)
        kk = jnp.concatenate([keys_scr[blk0 + g] for g in range(KC // LANES)], axis=1)
        s_abs = k0 + lax.broadcasted_iota(I32, (1, KC), 1)
        sel = (kk > thr) | ((kk == thr) & (s_abs <= jstar))
        sel = sel & cond_fn(s_abs)
        madd = jnp.where(sel, 0.0, NEG)
        s3 = s.reshape(H, T, KC) + madd[None]
        if bias is not None:
            s3 = s3 + bias
        s = s3.reshape(H * T, KC)
        m_prev = m_scr[...]
        m_new = jnp.maximum(m_prev, jnp.max(s, axis=1, keepdims=True))
        alpha = jnp.exp(m_prev - m_new)
        p = jnp.exp(s - jnp.concatenate([m_new] * (KC // LANES), axis=1))
        l_scr[...] = alpha * l_scr[...] + jnp.sum(p, axis=1, keepdims=True)
        acc_scr[...] = alpha * acc_scr[...] + _dot(p.astype(BF16), kv)
        m_scr[...] = m_new

    far_end = jnp.maximum(t0 - T, 0)
    n_far = (far_end + KC - 1) // KC

    def far_chunk(c, carry):
        k0 = pl.multiple_of(c * KC, KC)
        attend(c * (KC // LANES), k0, lambda s_abs: s_abs < far_end, None)
        return carry

    lax.fori_loop(0, n_far, far_chunk, 0)
    near0 = pl.multiple_of(far_end, LANES)
    attend(far_end // LANES, near0, lambda s_abs: s_abs <= t_abs, bias_ref[0])

    o = (acc_scr[...] / l_scr[...]).astype(BF16).reshape(H, T, A_KV_RANK)
    for p in range(H // 2):
        pair = jnp.concatenate([o[2 * p], o[2 * p + 1]], axis=1)
        o_ref[:, p * LANES:(p + 1) * LANES] = _dot(pair, wuv_ref[p]).astype(BF16)


def _dsa_attn(bsz, seq, qidx, widx, kidx, ckv, qlat, bias, wuvbd):
    T = DSA_T
    nq = seq // T
    n = bsz * seq
    topk = min(IDX_TOPK_MAX, seq // IDX_TOPK_FRAC)
    kernel = functools.partial(_dsa_attn_kernel, float(topk))
    return pl.pallas_call(
        kernel,
        grid=(bsz, nq),
        in_specs=[
            pl.BlockSpec((IDX_HEADS, T, IDX_DIM), lambda b, i: (0, b * nq + i, 0)),
            pl.BlockSpec((T, IDX_HEADS), lambda b, i: (b * nq + i, 0)),
            pl.BlockSpec((seq, IDX_DIM), lambda b, i: (b, 0)),
            pl.BlockSpec((seq, A_KV_RANK), lambda b, i: (b, 0)),
            pl.BlockSpec((A_HEADS, T, A_KV_RANK), lambda b, i: (0, b * nq + i, 0)),
            pl.BlockSpec((1, A_HEADS, T, DSA_KC), lambda b, i: (jnp.minimum(i, 1), 0, 0, 0)),
            pl.BlockSpec(wuvbd.shape, lambda b, i: (0, 0, 0)),
        ],
        out_specs=pl.BlockSpec((T, A_HEADS * A_V_DIM), lambda b, i: (b * nq + i, 0)),
        out_shape=jax.ShapeDtypeStruct((n, A_HEADS * A_V_DIM), BF16),
        scratch_shapes=[
            pltpu.VMEM((seq // LANES, T, LANES), I32),
            pltpu.VMEM((T, LANES), I32),
            pltpu.VMEM((A_HEADS * T, LANES), F32),
            pltpu.VMEM((A_HEADS * T, LANES), F32),
            pltpu.VMEM((A_HEADS * T, A_KV_RANK), F32),
        ],
        compiler_params=_cparams(("parallel", "arbitrary")),
        name="dsa_attn",
    )(qidx, widx, kidx, ckv, qlat, bias, wuvbd)


def _t5_bucket_np(dist):
    max_exact = REL_BUCKETS // 2
    n = np.maximum(dist, 0)
    ratio = np.log(np.maximum(n, 1).astype(np.float32) / np.float32(max_exact)) / np.float32(math.log(REL_MAX_DIST / max_exact))
    large = max_exact + (ratio * np.float32(REL_BUCKETS - max_exact)).astype(np.int32)
    large = np.minimum(large, REL_BUCKETS - 1)
    return np.where(n < max_exact, n, large)


def _dsa_bias_tiles(rel_table):
    T, KC = DSA_T, DSA_KC
    t = np.arange(T)[:, None]
    c = np.arange(KC)[None, :]
    d_first = t - c
    d_rest = t + T - c
    idx = np.stack([_t5_bucket_np(d_first), _t5_bucket_np(d_rest)]).astype(np.int32)
    far_bucket = int(_t5_bucket_np(np.array([T + 1]))[0])
    assert far_bucket == REL_BUCKETS - 1 and int(_t5_bucket_np(np.array([T - 15]))[0]) == far_bucket
    tiles = rel_table[jnp.asarray(idx)] - rel_table[far_bucket]
    return jnp.transpose(tiles, (0, 3, 1, 2)).astype(F32)


def _block_diag_pairs(w):
    h, a, b = w.shape
    z = jnp.zeros((h // 2, a, b), w.dtype)
    top = jnp.concatenate([w[0::2], z], axis=2)
    bot = jnp.concatenate([z, w[1::2]], axis=2)
    return jnp.concatenate([top, bot], axis=1)


def _dsa_mixer(x2, bsz, seq, w_in, g_q, g_kv, g_kidx, b_kidx, w_uq, w_uk, w_uv, w_qidx, rel_table):
    d = x2.shape[1]
    wcq = w_in[:, :A_Q_RANK].astype(BF16)
    wckv = w_in[:, A_Q_RANK:A_Q_RANK + A_KV_RANK].astype(BF16)
    wkw = jnp.pad(w_in[:, A_Q_RANK + A_KV_RANK:], ((0, 0), (0, LANES - IDX_DIM - IDX_HEADS))).astype(BF16)
    wuq = w_uq.reshape(A_Q_RANK, A_HEADS * A_HEAD_DIM).astype(BF16)
    wukbd = _block_diag_pairs(jnp.transpose(w_uk, (1, 2, 0))).astype(BF16)
    wuvbd = _block_diag_pairs(jnp.transpose(w_uv, (1, 0, 2))).astype(BF16)
    wqi = jnp.transpose(w_qidx, (1, 0, 2)).astype(BF16)
    row = lambda v: v.reshape(1, -1).astype(F32)
    ckv, kidx, widx, qlat, qidx = _dsa_proj(x2, wcq, wckv, wkw, row(g_q), row(g_kv), row(g_kidx), row(b_kidx),
                                            wuq, wukbd, wqi)
    bias = _dsa_bias_tiles(rel_table)
    return _dsa_attn(bsz, seq, qidx, widx, kidx, ckv, qlat, bias, wuvbd)


def _fox_proj_kernel(tiles_per_seq, x_ref, wq_ref, wk_ref, wv_ref, wft_ref, bf_ref,
                     q_ref, k_ref, v_ref, cum_ref, carry_scr):
    i = pl.program_id(0)
    tn = x_ref.shape[0]

    @pl.when(i % tiles_per_seq == 0)
    def _():
        carry_scr[...] = jnp.zeros(carry_scr.shape, F32)

    xb = x_ref[...].astype(BF16)
    q_ref[...] = (_dot(xb, wq_ref[...]) * (B_HEAD_DIM ** -0.5)).astype(BF16)
    k_ref[...] = _dot(xb, wk_ref[...]).astype(BF16)
    v_ref[...] = _dot(xb, wv_ref[...]).astype(BF16)
    z = _dot_t(wft_ref[...], xb) + bf_ref[...]
    logf = jnp.minimum(z, 0.0) - jnp.log(1.0 + jnp.exp(-jnp.abs(z)))
    r = lax.broadcasted_iota(I32, (tn, tn), 0)
    c = lax.broadcasted_iota(I32, (tn, tn), 1)
    tri = jnp.where(r <= c, 1.0, 0.0).astype(BF16)
    hi = logf.astype(BF16)
    lo = (logf - hi.astype(F32)).astype(BF16)
    cum = _dot(hi, tri) + _dot(lo, tri) + carry_scr[:, 0:1]
    cum_ref[...] = cum
    carry_scr[...] = jnp.broadcast_to(cum[:, tn - 1:tn], carry_scr.shape)


def _fox_proj(x2, seq, wq, wk, wv, wft, bf):
    n, d = x2.shape
    tn = min(TOK_TILE, seq)
    hd = B_HEADS * B_HEAD_DIM
    full = lambda a: pl.BlockSpec(a.shape, lambda i, _nd=a.ndim: (0,) * _nd)
    kernel = functools.partial(_fox_proj_kernel, seq // tn)
    return pl.pallas_call(
        kernel,
        grid=(n // tn,),
        in_specs=[pl.BlockSpec((tn, d), lambda i: (i, 0))] + [full(a) for a in (wq, wk, wv, wft, bf)],
        out_specs=[pl.BlockSpec((tn, hd), lambda i: (i, 0))] * 3 + [pl.BlockSpec((B_HEADS, tn), lambda i: (0, i))],
        out_shape=[jax.ShapeDtypeStruct((n, hd), BF16)] * 3 + [jax.ShapeDtypeStruct((B_HEADS, n), F32)],
        scratch_shapes=[pltpu.VMEM((B_HEADS, LANES), F32)],
        compiler_params=_cparams(("arbitrary",)),
        name="fox_proj",
    )(x2, wq, wk, wv, wft, bf)


def _fox_attn_kernel(q_ref, k_ref, v_ref, cum_ref, o_ref, m_scr, l_scr, acc_scr):
    T = FOX_T
    i = pl.program_id(2)
    lane = lax.broadcasted_iota(I32, (1, LANES), 1)
    lo_half = lane < B_HEAD_DIM
    q = q_ref[...]
    zero = jnp.zeros_like(q)
    q2 = (jnp.where(lo_half, q, zero), jnp.where(lo_half, zero, q))
    m_scr[...] = jnp.full(m_scr.shape, NEG, F32)
    l_scr[...] = jnp.zeros(l_scr.shape, F32)
    acc_scr[...] = jnp.zeros(acc_scr.shape, F32)

    def attend(k0, causal):
        kc = k_ref[pl.ds(k0, T), :]
        vc = v_ref[pl.ds(k0, T), :]
        cs = cum_ref[0, k0 // T]
        pv = []
        alphas = []
        for h in range(2):
            s = _dot_t(q2[h], kc) - cs[h:h + 1, :]
            if causal:
                r = lax.broadcasted_iota(I32, (T, T), 0)
                c = lax.broadcasted_iota(I32, (T, T), 1)
                s = jnp.where(c <= r, s, NEG)
            m_prev = m_scr[h]
            m_new = jnp.maximum(m_prev, jnp.max(s, axis=1, keepdims=True))
            alpha = jnp.exp(m_prev - m_new)
            p = jnp.exp(s - jnp.concatenate([m_new] * (T // LANES), axis=1))
            l_scr[h] = alpha * l_scr[h] + jnp.sum(p, axis=1, keepdims=True)
            m_scr[h] = m_new
            pv.append(_dot(p.astype(BF16), vc))
            alphas.append(alpha)
        acc_scr[...] = jnp.where(lo_half, alphas[0], alphas[1]) * acc_scr[...] + jnp.where(lo_half, pv[0], pv[1])

    def far(c, carry):
        attend(pl.multiple_of(c * T, T), False)
        return carry

    lax.fori_loop(0, i, far, 0)
    attend(pl.multiple_of(i * T, T), True)
    o_ref[...] = (acc_scr[...] / jnp.where(lo_half, l_scr[0], l_scr[1])).astype(BF16)


def _fox_attn(bsz, seq, q, k, v, cum3):
    T = FOX_T
    nq = seq // T
    n = bsz * seq
    pairs = B_HEADS // 2
    return pl.pallas_call(
        _fox_attn_kernel,
        grid=(bsz, pairs, nq),
        in_specs=[
            pl.BlockSpec((T, LANES), lambda b, p, i: (b * nq + i, p)),
            pl.BlockSpec((seq, LANES), lambda b, p, i: (b, p)),
            pl.BlockSpec((seq, LANES), lambda b, p, i: (b, p)),
            pl.BlockSpec((1, nq, 2, T), lambda b, p, i: (p, b, 0, 0)),
        ],
        out_specs=pl.BlockSpec((T, LANES), lambda b, p, i: (b * nq + i, p)),
        out_shape=jax.ShapeDtypeStruct((n, B_HEADS * B_HEAD_DIM), BF16),
        scratch_shapes=[
            pltpu.VMEM((2, T, LANES), F32),
            pltpu.VMEM((2, T, LANES), F32),
            pltpu.VMEM((T, LANES), F32),
        ],
        compiler_params=_cparams(("parallel", "parallel", "arbitrary")),
        name="fox_attn",
    )(q, k, v, cum3)


def _fox_mixer(x2, bsz, seq, w_in, b_f):
    hd = B_HEADS * B_HEAD_DIM
    wq = w_in[:, :hd].astype(BF16)
    wk = w_in[:, hd:2 * hd].astype(BF16)
    wv = w_in[:, 2 * hd:3 * hd].astype(BF16)
    wft = jnp.transpose(w_in[:, 3 * hd:]).astype(BF16)
    bf = b_f.reshape(B_HEADS, 1).astype(F32)
    q, k, v, cum = _fox_proj(x2, seq, wq, wk, wv, wft, bf)
    cum3 = jnp.transpose(cum.reshape(B_HEADS // 2, 2, bsz * seq // FOX_T, FOX_T), (0, 2, 1, 3))
    return _fox_attn(bsz, seq, q, k, v, cum3)


def _oproj_route_kernel(a_ref, x_ref, wo_ref, g_ref, b_ref, wr_ref, br_ref,
                        x1_ref, tope_ref, rank_ref, gate_ref, cnt_ref, carry_scr):
    i = pl.program_id(0)
    tn = x_ref.shape[0]
    ne = wr_ref.shape[1]

    @pl.when(i == 0)
    def _():
        carry_scr[...] = jnp.zeros(carry_scr.shape, F32)

    h = _dot(a_ref[...], wo_ref[...])
    x1 = _layer_norm(DEEPNORM_ALPHA * x_ref[...] + h, g_ref[...], b_ref[...])
    x1_ref[...] = x1
    logits = _dot(x1.astype(BF16), wr_ref[...]) + br_ref[...]
    lane = lax.broadcasted_iota(I32, (tn, ne), 1)
    work = logits
    vals, hots = [], []
    for k in range(TOP_K):
        m = jnp.max(work, axis=1, keepdims=True)
        idx = jnp.min(jnp.where(work == m, lane, ne), axis=1, keepdims=True)
        hot = lane == idx
        vals.append(m)
        hots.append(hot)
        tope_ref[:, k:k + 1] = idx
        work = jnp.where(hot, -jnp.inf, work)
    es = [jnp.exp(v - vals[0]) for v in vals]
    den = es[0] + es[1] + es[2] + es[3]
    for k in range(TOP_K):
        gate_ref[:, k:k + 1] = es[k] / den
    sel = jnp.zeros((tn, ne), F32)
    for hot in hots:
        sel = sel + jnp.where(hot, 1.0, 0.0)
    r = lax.broadcasted_iota(I32, (tn, tn), 0)
    c = lax.broadcasted_iota(I32, (tn, tn), 1)
    tril = jnp.where(c < r, 1.0, 0.0).astype(BF16)
    prefix = _dot(tril, sel.astype(BF16)) + carry_scr[0:1, :]
    for k in range(TOP_K):
        rank_ref[:, k:k + 1] = jnp.sum(jnp.where(hots[k], prefix, 0.0), axis=1, keepdims=True).astype(I32)
    total = carry_scr[0:1, :] + jnp.sum(sel, axis=0, keepdims=True)
    carry_scr[...] = jnp.broadcast_to(total, carry_scr.shape)
    cnt_ref[...] = total.astype(I32)


def _oproj_route(attn, x2, wo, g, b, wr, br):
    n, d = x2.shape
    tn = min(TOK_TILE, n)
    ne = wr.shape[1]
    full = lambda a: pl.BlockSpec(a.shape, lambda i, _nd=a.ndim: (0,) * _nd)
    tok = lambda w: pl.BlockSpec((tn, w), lambda i: (i, 0))
    return pl.pallas_call(
        _oproj_route_kernel,
        grid=(n // tn,),
        in_specs=[tok(attn.shape[1]), tok(d)] + [full(a) for a in (wo, g, b, wr, br)],
        out_specs=[tok(d), tok(TOP_K), tok(TOP_K), tok(TOP_K), pl.BlockSpec((1, ne), lambda i: (0, 0))],
        out_shape=[
            jax.ShapeDtypeStruct((n, d), F32),
            jax.ShapeDtypeStruct((n, TOP_K), I32),
            jax.ShapeDtypeStruct((n, TOP_K), I32),
            jax.ShapeDtypeStruct((n, TOP_K), F32),
            jax.ShapeDtypeStruct((1, ne), I32),
        ],
        scratch_shapes=[pltpu.VMEM((8, ne), F32)],
        compiler_params=_cparams(("arbitrary",)),
        name="oproj_route",
    )(attn, x2, wo, g, b, wr, br)


def _dispatch_kernel(dest_ref, x_hbm, xs_in_hbm, xs_hbm, sem):
    del xs_in_hbm
    i = pl.program_id(0)
    tn = dest_ref.shape[0] // TOP_K

    def issue(t, carry):
        tok = i * tn + t
        for k in range(TOP_K):
            d = dest_ref[t * TOP_K + k]
            pltpu.make_async_copy(x_hbm.at[pl.ds(tok, 1)], xs_hbm.at[pl.ds(d, 1)], sem).start()
        return carry

    lax.fori_loop(0, tn, issue, 0)
    pltpu.make_async_copy(x_hbm.at[pl.ds(0, tn * TOP_K)], xs_hbm.at[pl.ds(0, tn * TOP_K)], sem).wait()


def _dispatch(dest_flat, x1, n_slots):
    n, d = x1.shape
    tn = min(ROW_TILE, n)
    xs0 = jnp.zeros((n_slots, d), x1.dtype)
    return pl.pallas_call(
        _dispatch_kernel,
        grid=(n // tn,),
        in_specs=[
            pl.BlockSpec((tn * TOP_K,), lambda i: (i,), memory_space=pltpu.SMEM),
            pl.BlockSpec(memory_space=pl.ANY),
            pl.BlockSpec(memory_space=pl.ANY),
        ],
        out_specs=pl.BlockSpec(memory_space=pl.ANY),
        out_shape=jax.ShapeDtypeStruct((n_slots, d), x1.dtype),
        scratch_shapes=[pltpu.SemaphoreType.DMA(())],
        input_output_aliases={2: 0},
        compiler_params=pltpu.CompilerParams(dimension_semantics=("arbitrary",), has_side_effects=True),
        name="moe_dispatch",
    )(dest_flat, x1, xs0)


def _expert_kernel(te_ref, tv_ref, xs_ref, wgu_ref, bgu_ref, wd_ref, bd_ref, y_ref):
    t = pl.program_id(0)
    f = wd_ref.shape[1]

    @pl.when(tv_ref[t] > 0)
    def _():
        xb = xs_ref[...].astype(BF16)
        h = _dot(xb, wgu_ref[0]) + bgu_ref[0]
        gate = jnp.minimum(h[:, :f], SWIGLU_LIMIT)
        up = jnp.clip(h[:, f:], -SWIGLU_LIMIT, SWIGLU_LIMIT)
        act = (up + 1.0) * (gate * (1.0 / (1.0 + jnp.exp(-SWIGLU_ALPHA * gate))))
        y_ref[...] = _dot(act.astype(BF16), wd_ref[0]) + bd_ref[0]

    @pl.when(tv_ref[t] == 0)
    def _():
        y_ref[...] = jnp.zeros(y_ref.shape, F32)


def _experts(tile_e, tile_v, xs, wgu, bgu, wd, bd):
    n_slots, d = xs.shape
    tm = MOE_TM
    f2 = wgu.shape[2]
    f = wd.shape[1]
    grid_spec = pltpu.PrefetchScalarGridSpec(
        num_scalar_prefetch=2,
        grid=(n_slots // tm,),
        in_specs=[
            pl.BlockSpec((tm, d), lambda t, te, tv: (t, 0)),
            pl.BlockSpec((1, d, f2), lambda t, te, tv: (te[t], 0, 0)),
            pl.BlockSpec((1, 1, f2), lambda t, te, tv: (te[t], 0, 0)),
            pl.BlockSpec((1, f, d), lambda t, te, tv: (te[t], 0, 0)),
            pl.BlockSpec((1, 1, d), lambda t, te, tv: (te[t], 0, 0)),
        ],
        out_specs=pl.BlockSpec((tm, d), lambda t, te, tv: (t, 0)),
    )
    return pl.pallas_call(
        _expert_kernel,
        grid_spec=grid_spec,
        out_shape=jax.ShapeDtypeStruct((n_slots, d), F32),
        compiler_params=_cparams(("arbitrary",)),
        name="moe_experts",
    )(tile_e, tile_v, xs, wgu, bgu, wd, bd)


def _combine_kernel(dest_ref, y_hbm, gate_ref, x_ref, g_ref, b_ref, o_ref, ybuf, sem):
    tn = x_ref.shape[0]

    def issue(t, carry):
        for k in range(TOP_K):
            d = dest_ref[t * TOP_K + k]
            pltpu.make_async_copy(y_hbm.at[pl.ds(d, 1)], ybuf.at[k, pl.ds(t, 1)], sem).start()
        return carry

    lax.fori_loop(0, tn, issue, 0)
    for k in range(TOP_K):
        pltpu.make_async_copy(y_hbm.at[pl.ds(0, tn)], ybuf.at[k], sem).wait()
    gates = gate_ref[...]
    fsum = gates[:, 0:1] * ybuf[0]
    for k in range(1, TOP_K):
        fsum = fsum + gates[:, k:k + 1] * ybuf[k]
    o_ref[...] = _layer_norm(DEEPNORM_ALPHA * x_ref[...] + fsum, g_ref[...], b_ref[...])


def _combine(dest_flat, y, gates, x1, g, b):
    n, d = x1.shape
    tn = min(ROW_TILE, n)
    full = lambda a: pl.BlockSpec(a.shape, lambda i, _nd=a.ndim: (0,) * _nd)
    return pl.pallas_call(
        _combine_kernel,
        grid=(n // tn,),
        in_specs=[
            pl.BlockSpec((tn * TOP_K,), lambda i: (i,), memory_space=pltpu.SMEM),
            pl.BlockSpec(memory_space=pl.ANY),
            pl.BlockSpec((tn, TOP_K), lambda i: (i, 0)),
            pl.BlockSpec((tn, d), lambda i: (i, 0)),
            full(g), full(b),
        ],
        out_specs=pl.BlockSpec((tn, d), lambda i: (i, 0)),
        out_shape=jax.ShapeDtypeStruct((n, d), F32),
        scratch_shapes=[pltpu.VMEM((TOP_K, tn, d), F32), pltpu.SemaphoreType.DMA(())],
        compiler_params=_cparams(("arbitrary",)),
        name="moe_combine",
    )(dest_flat, y, gates, x1, g, b)


def _moe(x1, top_e, rank, gates, counts, w_gu, b_gu, w_down, b_down, ln_g, ln_b):
    n, d = x1.shape
    ne = w_gu.shape[0]
    tm = MOE_TM
    n_pairs = n * TOP_K
    n_slots = -(-n_pairs // tm) * tm + ne * tm
    n_tiles = n_slots // tm
    counts = counts.reshape(ne)
    padded = (counts + tm - 1) // tm * tm
    pad_ends = jnp.cumsum(padded)
    pad_starts = pad_ends - padded
    dest = (pad_starts[top_e] + rank).reshape(n_pairs).astype(I32)
    tile_start = jnp.arange(n_tiles, dtype=I32) * tm
    tile_v = (tile_start < pad_ends[ne - 1]).astype(I32)
    tile_e = jnp.minimum(jnp.searchsorted(pad_ends, tile_start, side="right"), ne - 1).astype(I32)
    xs = _dispatch(dest, x1, n_slots)
    y = _experts(tile_e, tile_v, xs, w_gu.astype(BF16), b_gu.reshape(ne, 1, -1).astype(F32),
                 w_down.astype(BF16), b_down.reshape(ne, 1, -1).astype(F32))
    return _combine(dest, y, gates, x1, ln_g, ln_b)


def kernel(x, rel_table, a_w_in, a_g_q, a_g_kv, a_g_kidx, a_b_kidx, a_w_uq, a_w_uk, a_w_uv, a_w_qidx, a_w_o,
           b_w_in, b_b_f, b_w_o, ln_mix_g, ln_mix_b, ln_ffn_g, ln_ffn_b, w_router, b_router, w_gu, b_gu,
           w_down, b_down):
    bsz, seq, d = x.shape
    n_mixers = 2
    depth = ln_mix_g.shape[0]
    x2 = x.reshape(bsz * seq, d)
    row = lambda v: v.reshape(1, -1).astype(F32)
    for i in range(depth):
        j = i // n_mixers
        if i % n_mixers == 0:
            attn = _dsa_mixer(x2, bsz, seq, a_w_in[j], a_g_q[j], a_g_kv[j], a_g_kidx[j], a_b_kidx[j],
                              a_w_uq[j], a_w_uk[j], a_w_uv[j], a_w_qidx[j], rel_table)
            wo = a_w_o[j]
        else:
            attn = _fox_mixer(x2, bsz, seq, b_w_in[j], b_b_f[j])
            wo = b_w_o[j]
        x1, top_e, rank, gates, counts = _oproj_route(attn, x2, wo.astype(BF16), row(ln_mix_g[i]), row(ln_mix_b[i]),
                                                      w_router[i].astype(BF16), row(b_router[i]))
        x2 = _moe(x1, top_e, rank, gates, counts, w_gu[i], b_gu[i], w_down[i], b_down[i],
                  row(ln_ffn_g[i]), row(ln_ffn_b[i]))
    return x2.reshape(bsz, seq, d)
```

```python
import functools
import math

import numpy as np
import jax
import jax.numpy as jnp
from jax import lax
from jax.experimental import pallas as pl
from jax.experimental.pallas import tpu as pltpu

F32 = jnp.float32
BF16 = jnp.bfloat16
I32 = jnp.int32

A_HEADS = 16
A_HEAD_DIM = 64
A_V_DIM = 64
A_Q_RANK = 256
A_KV_RANK = 128
IDX_HEADS = 8
IDX_DIM = 64
IDX_TOPK_MAX = 256
IDX_TOPK_FRAC = 4
B_HEADS = 16
B_HEAD_DIM = 64
REL_BUCKETS = 32
REL_MAX_DIST = 128
TOP_K = 4
SWIGLU_ALPHA = 1.702
SWIGLU_LIMIT = 7.0
LN_EPS = 1e-5
RMS_EPS = 1e-6
DEPTH = 2
DEEPNORM_ALPHA = (2.0 * DEPTH) ** 0.25

LANES = 128
NEG = -1e30
INT_MIN = -2147483648
MASK_KEY = INT_MIN
LOG2E = math.log2(math.e)
VMEM_LIMIT = 56 * 1024 * 1024

TOK_TILE = 512
DSA_T = 128
DSA_KC = 256
DSA_KF = 512
FOX_T = 512
MOE_TM = 512
ROW_TILE = 256


def _dot(a, b):
    return jnp.dot(a, b, preferred_element_type=F32)


def _dot_t(a, b):
    return lax.dot_general(a, b, (((1,), (1,)), ((), ())), preferred_element_type=F32)


def _cparams(sem):
    return pltpu.CompilerParams(dimension_semantics=sem, vmem_limit_bytes=VMEM_LIMIT)


def _layer_norm(v, g, b):
    mu = jnp.mean(v, axis=-1, keepdims=True)
    c = v - mu
    var = jnp.mean(c * c, axis=-1, keepdims=True)
    return c * lax.rsqrt(var + LN_EPS) * g + b


def _rms_norm(v, g):
    return v * lax.rsqrt(jnp.mean(v * v, axis=-1, keepdims=True) + RMS_EPS) * g


def _dsa_proj_kernel(x_ref, wcq_ref, wckv_ref, wkw_ref, gq_ref, gkv_ref, gki_ref, bki_ref,
                     wuq_ref, wuk_ref, wqi_ref,
                     ckv_ref, kidx_ref, widx_ref, qlat_ref, qidx_ref):
    xb = x_ref[...].astype(BF16)
    cq = _rms_norm(_dot(xb, wcq_ref[...]), gq_ref[...])
    ckv = _rms_norm(_dot(xb, wckv_ref[...]), gkv_ref[...])
    ckv_ref[...] = ckv.astype(BF16)
    kw = _dot(xb, wkw_ref[...])
    kidx = _layer_norm(kw[:, :IDX_DIM], gki_ref[...], bki_ref[...])
    kidx_ref[...] = kidx.astype(BF16)
    widx_ref[...] = kw[:, IDX_DIM:IDX_DIM + IDX_HEADS] * (IDX_HEADS ** -0.5 * IDX_DIM ** -0.5)
    cqb = cq.astype(BF16)
    qb = _dot(cqb, wuq_ref[...]).astype(BF16)
    scale = A_HEAD_DIM ** -0.5 * LOG2E
    for p in range(A_HEADS // 2):
        ql = _dot(qb[:, p * LANES:(p + 1) * LANES], wuk_ref[p]) * scale
        qlat_ref[2 * p] = ql[:, :A_KV_RANK].astype(BF16)
        qlat_ref[2 * p + 1] = ql[:, A_KV_RANK:].astype(BF16)
    for h in range(IDX_HEADS):
        qidx_ref[h] = _dot(cqb, wqi_ref[h]).astype(BF16)


def _dsa_proj(x2, wcq, wckv, wkw, gq, gkv, gki, bki, wuq, wukbd, wqi):
    n, d = x2.shape
    tn = min(TOK_TILE, n)
    full = lambda a: pl.BlockSpec(a.shape, lambda i, _nd=a.ndim: (0,) * _nd)
    return pl.pallas_call(
        _dsa_proj_kernel,
        grid=(n // tn,),
        in_specs=[pl.BlockSpec((tn, d), lambda i: (i, 0))] + [full(a) for a in (wcq, wckv, wkw, gq, gkv, gki, bki, wuq, wukbd, wqi)],
        out_specs=[
            pl.BlockSpec((tn, A_KV_RANK), lambda i: (i, 0)),
            pl.BlockSpec((tn, IDX_DIM), lambda i: (i, 0)),
            pl.BlockSpec((tn, IDX_HEADS), lambda i: (i, 0)),
            pl.BlockSpec((A_HEADS, tn, A_KV_RANK), lambda i: (0, i, 0)),
            pl.BlockSpec((IDX_HEADS, tn, IDX_DIM), lambda i: (0, i, 0)),
        ],
        out_shape=[
            jax.ShapeDtypeStruct((n, A_KV_RANK), BF16),
            jax.ShapeDtypeStruct((n, IDX_DIM), BF16),
            jax.ShapeDtypeStruct((n, IDX_HEADS), F32),
            jax.ShapeDtypeStruct((A_HEADS, n, A_KV_RANK), BF16),
            jax.ShapeDtypeStruct((IDX_HEADS, n, IDX_DIM), BF16),
        ],
        compiler_params=_cparams(("parallel",)),
        name="dsa_proj",
    )(x2, wcq, wckv, wkw, gq, gkv, gki, bki, wuq, wukbd, wqi)


def _dsa_attn_kernel(topk, qidx_ref, widx_ref, kidx_ref, ckv_ref, qlat_ref, bias_ref, wuv_ref, o_ref,
                     keys_scr, jstar_scr, m_scr, l_scr, acc_scr):
    T, KC, KF, H = DSA_T, DSA_KC, DSA_KF, A_HEADS
    i = pl.program_id(1)
    t0 = i * T
    n_blk = i + 1
    n_sc = (n_blk * T + KC - 1) // KC
    t_abs = t0 + lax.broadcasted_iota(I32, (T, 1), 0)
    w = widx_ref[...]

    def score_chunk(c, carry):
        k0 = pl.multiple_of(c * KC, KC)
        kc = kidx_ref[pl.ds(k0, KC), :]
        acc = jnp.zeros((T, KC), F32)
        for h in range(IDX_HEADS):
            acc = acc + w[:, h:h + 1] * jnp.maximum(_dot_t(qidx_ref[h], kc), 0.0)
        acc = jnp.where(acc == 0.0, 0.0, acc)
        bits = pltpu.bitcast(acc, I32)
        key = bits ^ ((bits >> 31) & 0x7FFFFFFF)
        s_abs = k0 + lax.broadcasted_iota(I32, (1, KC), 1)
        key = jnp.where(s_abs <= t_abs, key, MASK_KEY)
        for g in range(KC // LANES):
            keys_scr[c * (KC // LANES) + g] = key[:, g * LANES:(g + 1) * LANES]
        return carry

    lax.fori_loop(0, n_sc, score_chunk, 0)

    def count_ge(cand_s):
        cand_b = jnp.broadcast_to(cand_s, (T, LANES))

        def body(j, part):
            part = part + jnp.where(keys_scr[2 * j] >= cand_b, 1.0, 0.0)
            return part + jnp.where(keys_scr[2 * j + 1] >= cand_b, 1.0, 0.0)

        part = lax.fori_loop(0, n_sc, body, jnp.zeros((T, LANES), F32))
        return jnp.sum(part, axis=1, keepdims=True)

    def bit_body(bi, ans):
        cand = ans | jnp.left_shift(jnp.int32(1), 31 - bi)
        cnt = count_ge(cand ^ INT_MIN)
        return jnp.where(cnt >= topk, cand, ans)

    ans = lax.fori_loop(0, 32, bit_body, jnp.zeros((T, 1), I32))
    thr = ans ^ INT_MIN
    thr_b = jnp.broadcast_to(thr, (T, LANES))

    def count_gt_eq():
        def body(j, parts):
            kk = keys_scr[j]
            return (parts[0] + jnp.where(kk > thr_b, 1.0, 0.0), parts[1] + jnp.where(kk == thr_b, 1.0, 0.0))

        z = jnp.zeros((T, LANES), F32)
        pg, pe = lax.fori_loop(0, n_blk, body, (z, z))
        return jnp.sum(pg, axis=1, keepdims=True), jnp.sum(pe, axis=1, keepdims=True)

    cnt_gt, cnt_eq = count_gt_eq()
    need = topk - cnt_gt
    s_total = keys_scr.shape[0] * LANES
    jstar_scr[...] = jnp.full((T, LANES), s_total, I32)
    tie = jnp.max(jnp.where(cnt_gt + cnt_eq > topk, 1.0, 0.0)) > 0.0

    @pl.when(tie)
    def _():
        n_bits = max(1, (s_total - 1).bit_length())

        def jbit(bi, lo):
            cand = lo + jnp.left_shift(jnp.int32(1), n_bits - 1 - bi)
            cand_b = jnp.broadcast_to(cand, (T, LANES))

            def body(j, part):
                pos = j * LANES + lax.broadcasted_iota(I32, (T, LANES), 1)
                hit = jnp.where(keys_scr[j] == thr_b, 1.0, 0.0)
                return part + jnp.where(pos < cand_b, hit, 0.0)

            part = lax.fori_loop(0, n_blk, body, jnp.zeros((T, LANES), F32))
            c = jnp.sum(part, axis=1, keepdims=True)
            return jnp.where(c < need, cand, lo)

        lo = lax.fori_loop(0, n_bits, jbit, jnp.zeros((T, 1), I32))
        jstar_scr[...] = jnp.broadcast_to(lo, (T, LANES))

    jstar = jstar_scr[:, 0:1]

    m_scr[...] = jnp.full(m_scr.shape, NEG, F32)
    l_scr[...] = jnp.zeros(l_scr.shape, F32)
    acc_scr[...] = jnp.zeros(acc_scr.shape, F32)
    q = qlat_ref[...].reshape(H * T, A_KV_RANK)

    def attend(blk0, k0, kw, cond_fn, bias):
        kv = ckv_ref[pl.ds(k0, kw), :]
        s = _dot_t(q, kv)
        kk = jnp.concatenate([keys_scr[blk0 + g] for g in range(kw // LANES)], axis=1)
        s_abs = k0 + lax.broadcasted_iota(I32, (1, kw), 1)
        sel = (kk > thr) | ((kk == thr) & (s_abs <= jstar))
        sel = sel & cond_fn(s_abs)
        madd = jnp.where(sel, 0.0, NEG)
        s3 = s.reshape(H, T, kw) + madd[None]
        if bias is not None:
            s3 = s3 + bias
        s = s3.reshape(H * T, kw)
        m_prev = m_scr[...]
        m_new = jnp.maximum(m_prev, jnp.max(s, axis=1, keepdims=True))
        alpha = jnp.exp2(m_prev - m_new)
        p = jnp.exp2(s - jnp.concatenate([m_new] * (kw // LANES), axis=1))
        l_scr[...] = alpha * l_scr[...] + jnp.sum(p, axis=1, keepdims=True)
        acc_scr[...] = alpha * acc_scr[...] + _dot(p.astype(BF16), kv)
        m_scr[...] = m_new

    far_end = jnp.maximum(t0 - T, 0)
    n_far = (far_end + KF - 1) // KF

    def far_chunk(c, carry):
        k0 = pl.multiple_of(c * KF, KF)
        attend(c * (KF // LANES), k0, KF, lambda s_abs: s_abs < far_end, None)
        return carry

    lax.fori_loop(0, n_far, far_chunk, 0)
    near0 = pl.multiple_of(far_end, LANES)
    attend(far_end // LANES, near0, KC, lambda s_abs: s_abs <= t_abs, bias_ref[0])

    o = (acc_scr[...] / l_scr[...]).astype(BF16).reshape(H, T, A_KV_RANK)
    for p in range(H // 2):
        pair = jnp.concatenate([o[2 * p], o[2 * p + 1]], axis=1)
        o_ref[:, p * LANES:(p + 1) * LANES] = _dot(pair, wuv_ref[p]).astype(BF16)


def _dsa_attn(bsz, seq, qidx, widx, kidx, ckv, qlat, bias, wuvbd):
    T = DSA_T
    nq = seq // T
    n = bsz * seq
    topk = min(IDX_TOPK_MAX, seq // IDX_TOPK_FRAC)
    kernel = functools.partial(_dsa_attn_kernel, float(topk))
    return pl.pallas_call(
        kernel,
        grid=(bsz, nq),
        in_specs=[
            pl.BlockSpec((IDX_HEADS, T, IDX_DIM), lambda b, i: (0, b * nq + i, 0)),
            pl.BlockSpec((T, IDX_HEADS), lambda b, i: (b * nq + i, 0)),
            pl.BlockSpec((seq, IDX_DIM), lambda b, i: (b, 0)),
            pl.BlockSpec((seq, A_KV_RANK), lambda b, i: (b, 0)),
            pl.BlockSpec((A_HEADS, T, A_KV_RANK), lambda b, i: (0, b * nq + i, 0)),
            pl.BlockSpec((1, A_HEADS, T, DSA_KC), lambda b, i: (jnp.minimum(i, 1), 0, 0, 0)),
            pl.BlockSpec(wuvbd.shape, lambda b, i: (0, 0, 0)),
        ],
        out_specs=pl.BlockSpec((T, A_HEADS * A_V_DIM), lambda b, i: (b * nq + i, 0)),
        out_shape=jax.ShapeDtypeStruct((n, A_HEADS * A_V_DIM), BF16),
        scratch_shapes=[
            pltpu.VMEM((seq // LANES, T, LANES), I32),
            pltpu.VMEM((T, LANES), I32),
            pltpu.VMEM((A_HEADS * T, LANES), F32),
            pltpu.VMEM((A_HEADS * T, LANES), F32),
            pltpu.VMEM((A_HEADS * T, A_KV_RANK), F32),
        ],
        compiler_params=_cparams(("parallel", "arbitrary")),
        name="dsa_attn",
    )(qidx, widx, kidx, ckv, qlat, bias, wuvbd)


def _t5_bucket_np(dist):
    max_exact = REL_BUCKETS // 2
    n = np.maximum(dist, 0)
    ratio = np.log(np.maximum(n, 1).astype(np.float32) / np.float32(max_exact)) / np.float32(math.log(REL_MAX_DIST / max_exact))
    large = max_exact + (ratio * np.float32(REL_BUCKETS - max_exact)).astype(np.int32)
    large = np.minimum(large, REL_BUCKETS - 1)
    return np.where(n < max_exact, n, large)


def _dsa_bias_tiles(rel_table):
    T, KC = DSA_T, DSA_KC
    t = np.arange(T)[:, None]
    c = np.arange(KC)[None, :]
    d_first = t - c
    d_rest = t + T - c
    idx = np.stack([_t5_bucket_np(d_first), _t5_bucket_np(d_rest)]).astype(np.int32)
    far_bucket = int(_t5_bucket_np(np.array([T + 1]))[0])
    assert far_bucket == REL_BUCKETS - 1 and int(_t5_bucket_np(np.array([T - 15]))[0]) == far_bucket
    tiles = (rel_table[jnp.asarray(idx)] - rel_table[far_bucket]) * LOG2E
    return jnp.transpose(tiles, (0, 3, 1, 2)).astype(F32)


def _block_diag_pairs(w):
    h, a, b = w.shape
    z = jnp.zeros((h // 2, a, b), w.dtype)
    top = jnp.concatenate([w[0::2], z], axis=2)
    bot = jnp.concatenate([z, w[1::2]], axis=2)
    return jnp.concatenate([top, bot], axis=1)


def _dsa_mixer(x2, bsz, seq, w_in, g_q, g_kv, g_kidx, b_kidx, w_uq, w_uk, w_uv, w_qidx, rel_table):
    d = x2.shape[1]
    wcq = w_in[:, :A_Q_RANK].astype(BF16)
    wckv = w_in[:, A_Q_RANK:A_Q_RANK + A_KV_RANK].astype(BF16)
    wkw = jnp.pad(w_in[:, A_Q_RANK + A_KV_RANK:], ((0, 0), (0, LANES - IDX_DIM - IDX_HEADS))).astype(BF16)
    wuq = w_uq.reshape(A_Q_RANK, A_HEADS * A_HEAD_DIM).astype(BF16)
    wukbd = _block_diag_pairs(jnp.transpose(w_uk, (1, 2, 0))).astype(BF16)
    wuvbd = _block_diag_pairs(jnp.transpose(w_uv, (1, 0, 2))).astype(BF16)
    wqi = jnp.transpose(w_qidx, (1, 0, 2)).astype(BF16)
    row = lambda v: v.reshape(1, -1).astype(F32)
    ckv, kidx, widx, qlat, qidx = _dsa_proj(x2, wcq, wckv, wkw, row(g_q), row(g_kv), row(g_kidx), row(b_kidx),
                                            wuq, wukbd, wqi)
    bias = _dsa_bias_tiles(rel_table)
    return _dsa_attn(bsz, seq, qidx, widx, kidx, ckv, qlat, bias, wuvbd)


def _fox_proj_kernel(tiles_per_seq, x_ref, wq_ref, wk_ref, wv_ref, wft_ref, bf_ref,
                     q_ref, k_ref, v_ref, cum_ref, carry_scr):
    i = pl.program_id(0)
    tn = x_ref.shape[0]

    @pl.when(i % tiles_per_seq == 0)
    def _():
        carry_scr[...] = jnp.zeros(carry_scr.shape, F32)

    xb = x_ref[...].astype(BF16)
    q_ref[...] = (_dot(xb, wq_ref[...]) * (B_HEAD_DIM ** -0.5 * LOG2E)).astype(BF16)
    k_ref[...] = _dot(xb, wk_ref[...]).astype(BF16)
    v_ref[...] = _dot(xb, wv_ref[...]).astype(BF16)
    z = _dot_t(wft_ref[...], xb) + bf_ref[...]
    logf = jnp.minimum(z, 0.0) - jnp.log(1.0 + jnp.exp(-jnp.abs(z)))
    r = lax.broadcasted_iota(I32, (tn, tn), 0)
    c = lax.broadcasted_iota(I32, (tn, tn), 1)
    tri = jnp.where(r <= c, 1.0, 0.0).astype(BF16)
    hi = logf.astype(BF16)
    lo = (logf - hi.astype(F32)).astype(BF16)
    cum = _dot(hi, tri) + _dot(lo, tri) + carry_scr[:, 0:1]
    cum_ref[...] = cum * LOG2E
    carry_scr[...] = jnp.broadcast_to(cum[:, tn - 1:tn], carry_scr.shape)


def _fox_proj(x2, seq, wq, wk, wv, wft, bf):
    n, d = x2.shape
    tn = min(TOK_TILE, seq)
    hd = B_HEADS * B_HEAD_DIM
    full = lambda a: pl.BlockSpec(a.shape, lambda i, _nd=a.ndim: (0,) * _nd)
    kernel = functools.partial(_fox_proj_kernel, seq // tn)
    return pl.pallas_call(
        kernel,
        grid=(n // tn,),
        in_specs=[pl.BlockSpec((tn, d), lambda i: (i, 0))] + [full(a) for a in (wq, wk, wv, wft, bf)],
        out_specs=[pl.BlockSpec((tn, hd), lambda i: (i, 0))] * 3 + [pl.BlockSpec((B_HEADS, tn), lambda i: (0, i))],
        out_shape=[jax.ShapeDtypeStruct((n, hd), BF16)] * 3 + [jax.ShapeDtypeStruct((B_HEADS, n), F32)],
        scratch_shapes=[pltpu.VMEM((B_HEADS, LANES), F32)],
        compiler_params=_cparams(("arbitrary",)),
        name="fox_proj",
    )(x2, wq, wk, wv, wft, bf)


def _fox_attn_kernel(q_ref, k_ref, v_ref, cum_ref, o_ref, m_scr, l_scr, acc_scr):
    T = FOX_T
    i = pl.program_id(2)
    lane = lax.broadcasted_iota(I32, (1, LANES), 1)
    lo_half = lane < B_HEAD_DIM
    q = q_ref[...]
    zero = jnp.zeros_like(q)
    q2 = (jnp.where(lo_half, q, zero), jnp.where(lo_half, zero, q))
    m_scr[...] = jnp.full(m_scr.shape, NEG, F32)
    l_scr[...] = jnp.zeros(l_scr.shape, F32)
    acc_scr[...] = jnp.zeros(acc_scr.shape, F32)

    def attend(k0, causal):
        kc = k_ref[pl.ds(k0, T), :]
        vc = v_ref[pl.ds(k0, T), :]
        cs = cum_ref[0, k0 // T]
        pv = []
        alphas = []
        for h in range(2):
            s = _dot_t(q2[h], kc) - cs[h:h + 1, :]
            if causal:
                r = lax.broadcasted_iota(I32, (T, T), 0)
                c = lax.broadcasted_iota(I32, (T, T), 1)
                s = jnp.where(c <= r, s, NEG)
            m_prev = m_scr[h]
            m_new = jnp.maximum(m_prev, jnp.max(s, axis=1, keepdims=True))
            alpha = jnp.exp2(m_prev - m_new)
            p = jnp.exp2(s - jnp.concatenate([m_new] * (T // LANES), axis=1))
            l_scr[h] = alpha * l_scr[h] + jnp.sum(p, axis=1, keepdims=True)
            m_scr[h] = m_new
            pv.append(_dot(p.astype(BF16), vc))
            alphas.append(alpha)
        acc_scr[...] = jnp.where(lo_half, alphas[0], alphas[1]) * acc_scr[...] + jnp.where(lo_half, pv[0], pv[1])

    def far(c, carry):
        attend(pl.multiple_of(c * T, T), False)
        return carry

    lax.fori_loop(0, i, far, 0)
    attend(pl.multiple_of(i * T, T), True)
    o_ref[...] = (acc_scr[...] / jnp.where(lo_half, l_scr[0], l_scr[1])).astype(BF16)


def _fox_attn(bsz, seq, q, k, v, cum3):
    T = FOX_T
    nq = seq // T
    n = bsz * seq
    pairs = B_HEADS // 2
    return pl.pallas_call(
        _fox_attn_kernel,
        grid=(bsz, pairs, nq),
        in_specs=[
            pl.BlockSpec((T, LANES), lambda b, p, i: (b * nq + i, p)),
            pl.BlockSpec((seq, LANES), lambda b, p, i: (b, p)),
            pl.BlockSpec((seq, LANES), lambda b, p, i: (b, p)),
            pl.BlockSpec((1, nq, 2, T), lambda b, p, i: (p, b, 0, 0)),
        ],
        out_specs=pl.BlockSpec((T, LANES), lambda b, p, i: (b * nq + i, p)),
        out_shape=jax.ShapeDtypeStruct((n, B_HEADS * B_HEAD_DIM), BF16),
        scratch_shapes=[
            pltpu.VMEM((2, T, LANES), F32),
            pltpu.VMEM((2, T, LANES), F32),
            pltpu.VMEM((T, LANES), F32),
        ],
        compiler_params=_cparams(("parallel", "parallel", "arbitrary")),
        name="fox_attn",
    )(q, k, v, cum3)


def _fox_mixer(x2, bsz, seq, w_in, b_f):
    hd = B_HEADS * B_HEAD_DIM
    wq = w_in[:, :hd].astype(BF16)
    wk = w_in[:, hd:2 * hd].astype(BF16)
    wv = w_in[:, 2 * hd:3 * hd].astype(BF16)
    wft = jnp.transpose(w_in[:, 3 * hd:]).astype(BF16)
    bf = b_f.reshape(B_HEADS, 1).astype(F32)
    q, k, v, cum = _fox_proj(x2, seq, wq, wk, wv, wft, bf)
    cum3 = jnp.transpose(cum.reshape(B_HEADS // 2, 2, bsz * seq // FOX_T, FOX_T), (0, 2, 1, 3))
    return _fox_attn(bsz, seq, q, k, v, cum3)


def _oproj_route_kernel(a_ref, x_ref, wo_ref, g_ref, b_ref, wr_ref, br_ref,
                        x1_ref, tope_ref, rank_ref, gate_ref, cnt_ref, carry_scr):
    i = pl.program_id(0)
    tn = x_ref.shape[0]
    ne = wr_ref.shape[1]

    @pl.when(i == 0)
    def _():
        carry_scr[...] = jnp.zeros(carry_scr.shape, F32)

    h = _dot(a_ref[...], wo_ref[...])
    x1 = _layer_norm(DEEPNORM_ALPHA * x_ref[...] + h, g_ref[...], b_ref[...])
    x1_ref[...] = x1
    logits = _dot(x1.astype(BF16), wr_ref[...]) + br_ref[...]
    lane = lax.broadcasted_iota(I32, (tn, ne), 1)
    work = logits
    vals, hots = [], []
    for k in range(TOP_K):
        m = jnp.max(work, axis=1, keepdims=True)
        idx = jnp.min(jnp.where(work == m, lane, ne), axis=1, keepdims=True)
        hot = lane == idx
        vals.append(m)
        hots.append(hot)
        tope_ref[:, k:k + 1] = idx
        work = jnp.where(hot, -jnp.inf, work)
    es = [jnp.exp(v - vals[0]) for v in vals]
    den = es[0] + es[1] + es[2] + es[3]
    for k in range(TOP_K):
        gate_ref[:, k:k + 1] = es[k] / den
    sel = jnp.zeros((tn, ne), F32)
    for hot in hots:
        sel = sel + jnp.where(hot, 1.0, 0.0)
    r = lax.broadcasted_iota(I32, (tn, tn), 0)
    c = lax.broadcasted_iota(I32, (tn, tn), 1)
    tril = jnp.where(c < r, 1.0, 0.0).astype(BF16)
    prefix = _dot(tril, sel.astype(BF16)) + carry_scr[0:1, :]
    for k in range(TOP_K):
        rank_ref[:, k:k + 1] = jnp.sum(jnp.where(hots[k], prefix, 0.0), axis=1, keepdims=True).astype(I32)
    total = carry_scr[0:1, :] + jnp.sum(sel, axis=0, keepdims=True)
    carry_scr[...] = jnp.broadcast_to(total, carry_scr.shape)
    cnt_ref[...] = total.astype(I32)


def _oproj_route(attn, x2, wo, g, b, wr, br):
    n, d = x2.shape
    tn = min(TOK_TILE, n)
    ne = wr.shape[1]
    full = lambda a: pl.BlockSpec(a.shape, lambda i, _nd=a.ndim: (0,) * _nd)
    tok = lambda w: pl.BlockSpec((tn, w), lambda i: (i, 0))
    return pl.pallas_call(
        _oproj_route_kernel,
        grid=(n // tn,),
        in_specs=[tok(attn.shape[1]), tok(d)] + [full(a) for a in (wo, g, b, wr, br)],
        out_specs=[tok(d), tok(TOP_K), tok(TOP_K), tok(TOP_K), pl.BlockSpec((1, ne), lambda i: (0, 0))],
        out_shape=[
            jax.ShapeDtypeStruct((n, d), F32),
            jax.ShapeDtypeStruct((n, TOP_K), I32),
            jax.ShapeDtypeStruct((n, TOP_K), I32),
            jax.ShapeDtypeStruct((n, TOP_K), F32),
            jax.ShapeDtypeStruct((1, ne), I32),
        ],
        scratch_shapes=[pltpu.VMEM((8, ne), F32)],
        compiler_params=_cparams(("arbitrary",)),
        name="oproj_route",
    )(attn, x2, wo, g, b, wr, br)


def _dispatch_kernel(dest_ref, x_ref, xs_in_hbm, xs_hbm, sem):
    del xs_in_hbm
    tn = x_ref.shape[0]

    def issue(t, carry):
        for k in range(TOP_K):
            d = dest_ref[t * TOP_K + k]
            pltpu.make_async_copy(x_ref.at[pl.ds(t, 1)], xs_hbm.at[pl.ds(d, 1)], sem).start()
        return carry

    lax.fori_loop(0, tn, issue, 0)
    for k in range(TOP_K):
        pltpu.make_async_copy(x_ref, xs_hbm.at[pl.ds(0, tn)], sem).wait()


def _dispatch(dest_flat, x1, n_slots):
    n, d = x1.shape
    tn = min(ROW_TILE, n)
    xs0 = jnp.zeros((n_slots, d), x1.dtype)
    return pl.pallas_call(
        _dispatch_kernel,
        grid=(n // tn,),
        in_specs=[
            pl.BlockSpec((tn * TOP_K,), lambda i: (i,), memory_space=pltpu.SMEM),
            pl.BlockSpec((tn, d), lambda i: (i, 0)),
            pl.BlockSpec(memory_space=pl.ANY),
        ],
        out_specs=pl.BlockSpec(memory_space=pl.ANY),
        out_shape=jax.ShapeDtypeStruct((n_slots, d), x1.dtype),
        scratch_shapes=[pltpu.SemaphoreType.DMA(())],
        input_output_aliases={2: 0},
        compiler_params=pltpu.CompilerParams(dimension_semantics=("arbitrary",), has_side_effects=True),
        name="moe_dispatch",
    )(dest_flat, x1, xs0)


def _expert_kernel(te_ref, tv_ref, xs_ref, wgu_ref, bgu_ref, wd_ref, bd_ref, y_ref):
    t = pl.program_id(0)
    f = wd_ref.shape[1]

    @pl.when(tv_ref[t] > 0)
    def _():
        xb = xs_ref[...].astype(BF16)
        h = _dot(xb, wgu_ref[0]) + bgu_ref[0]
        gate = jnp.minimum(h[:, :f], SWIGLU_LIMIT)
        up = jnp.clip(h[:, f:], -SWIGLU_LIMIT, SWIGLU_LIMIT)
        act = (up + 1.0) * (gate * (1.0 / (1.0 + jnp.exp(-SWIGLU_ALPHA * gate))))
        y_ref[...] = _dot(act.astype(BF16), wd_ref[0]) + bd_ref[0]

    @pl.when(tv_ref[t] == 0)
    def _():
        y_ref[...] = jnp.zeros(y_ref.shape, F32)


def _experts(tile_e, tile_v, xs, wgu, bgu, wd, bd):
    n_slots, d = xs.shape
    tm = MOE_TM
    f2 = wgu.shape[2]
    f = wd.shape[1]
    grid_spec = pltpu.PrefetchScalarGridSpec(
        num_scalar_prefetch=2,
        grid=(n_slots // tm,),
        in_specs=[
            pl.BlockSpec((tm, d), lambda t, te, tv: (t, 0)),
            pl.BlockSpec((1, d, f2), lambda t, te, tv: (te[t], 0, 0)),
            pl.BlockSpec((1, 1, f2), lambda t, te, tv: (te[t], 0, 0)),
            pl.BlockSpec((1, f, d), lambda t, te, tv: (te[t], 0, 0)),
            pl.BlockSpec((1, 1, d), lambda t, te, tv: (te[t], 0, 0)),
        ],
        out_specs=pl.BlockSpec((tm, d), lambda t, te, tv: (t, 0)),
    )
    return pl.pallas_call(
        _expert_kernel,
        grid_spec=grid_spec,
        out_shape=jax.ShapeDtypeStruct((n_slots, d), F32),
        compiler_params=_cparams(("arbitrary",)),
        name="moe_experts",
    )(tile_e, tile_v, xs, wgu, bgu, wd, bd)


def _combine_kernel(dest_ref, y_hbm, gate_ref, x_ref, g_ref, b_ref, o_ref, ybuf, sem):
    tn = x_ref.shape[0]

    def issue(t, carry):
        for k in range(TOP_K):
            d = dest_ref[t * TOP_K + k]
            pltpu.make_async_copy(y_hbm.at[pl.ds(d, 1)], ybuf.at[k, pl.ds(t, 1)], sem).start()
        return carry

    lax.fori_loop(0, tn, issue, 0)
    for k in range(TOP_K):
        pltpu.make_async_copy(y_hbm.at[pl.ds(0, tn)], ybuf.at[k], sem).wait()
    gates = gate_ref[...]
    fsum = gates[:, 0:1] * ybuf[0]
    for k in range(1, TOP_K):
        fsum = fsum + gates[:, k:k + 1] * ybuf[k]
    o_ref[...] = _layer_norm(DEEPNORM_ALPHA * x_ref[...] + fsum, g_ref[...], b_ref[...])


def _combine(dest_flat, y, gates, x1, g, b):
    n, d = x1.shape
    tn = min(ROW_TILE, n)
    full = lambda a: pl.BlockSpec(a.shape, lambda i, _nd=a.ndim: (0,) * _nd)
    return pl.pallas_call(
        _combine_kernel,
        grid=(n // tn,),
        in_specs=[
            pl.BlockSpec((tn * TOP_K,), lambda i: (i,), memory_space=pltpu.SMEM),
            pl.BlockSpec(memory_space=pl.ANY),
            pl.BlockSpec((tn, TOP_K), lambda i: (i, 0)),
            pl.BlockSpec((tn, d), lambda i: (i, 0)),
            full(g), full(b),
        ],
        out_specs=pl.BlockSpec((tn, d), lambda i: (i, 0)),
        out_shape=jax.ShapeDtypeStruct((n, d), F32),
        scratch_shapes=[pltpu.VMEM((TOP_K, tn, d), F32), pltpu.SemaphoreType.DMA(())],
        compiler_params=_cparams(("arbitrary",)),
        name="moe_combine",
    )(dest_flat, y, gates, x1, g, b)


def _moe(x1, top_e, rank, gates, counts, w_gu, b_gu, w_down, b_down, ln_g, ln_b):
    n, d = x1.shape
    ne = w_gu.shape[0]
    tm = MOE_TM
    n_pairs = n * TOP_K
    n_slots = -(-n_pairs // tm) * tm + ne * tm
    n_tiles = n_slots // tm
    counts = counts.reshape(ne)
    padded = (counts + tm - 1) // tm * tm
    pad_ends = jnp.cumsum(padded)
    pad_starts = pad_ends - padded
    dest = (pad_starts[top_e] + rank).reshape(n_pairs).astype(I32)
    tile_start = jnp.arange(n_tiles, dtype=I32) * tm
    tile_v = (tile_start < pad_ends[ne - 1]).astype(I32)
    tile_e = jnp.minimum(jnp.searchsorted(pad_ends, tile_start, side="right"), ne - 1).astype(I32)
    xs = _dispatch(dest, x1, n_slots)
    y = _experts(tile_e, tile_v, xs, w_gu.astype(BF16), b_gu.reshape(ne, 1, -1).astype(F32),
                 w_down.astype(BF16), b_down.reshape(ne, 1, -1).astype(F32))
    return _combine(dest, y, gates, x1, ln_g, ln_b)


def kernel(x, rel_table, a_w_in, a_g_q, a_g_kv, a_g_kidx, a_b_kidx, a_w_uq, a_w_uk, a_w_uv, a_w_qidx, a_w_o,
           b_w_in, b_b_f, b_w_o, ln_mix_g, ln_mix_b, ln_ffn_g, ln_ffn_b, w_router, b_router, w_gu, b_gu,
           w_down, b_down):
    bsz, seq, d = x.shape
    n_mixers = 2
    depth = ln_mix_g.shape[0]
    x2 = x.reshape(bsz * seq, d)
    row = lambda v: v.reshape(1, -1).astype(F32)
    for i in range(depth):
        j = i // n_mixers
        if i % n_mixers == 0:
            attn = _dsa_mixer(x2, bsz, seq, a_w_in[j], a_g_q[j], a_g_kv[j], a_g_kidx[j], a_b_kidx[j],
                              a_w_uq[j], a_w_uk[j], a_w_uv[j], a_w_qidx[j], rel_table)
            wo = a_w_o[j]
        else:
            attn = _fox_mixer(x2, bsz, seq, b_w_in[j], b_b_f[j])
            wo = b_w_o[j]
        x1, top_e, rank, gates, counts = _oproj_route(attn, x2, wo.astype(BF16), row(ln_mix_g[i]), row(ln_mix_b[i]),
                                                      w_router[i].astype(BF16), row(b_router[i]))
        x2 = _moe(x1, top_e, rank, gates, counts, w_gu[i], b_gu[i], w_down[i], b_down[i],
                  row(ln_ffn_g[i]), row(ln_ffn_b[i]))
    return x2.reshape(bsz, seq, d)
```

```python
import functools
import math

import numpy as np
import jax
import jax.numpy as jnp
from jax import lax
from jax.experimental import pallas as pl
from jax.experimental.pallas import tpu as pltpu

F32 = jnp.float32
BF16 = jnp.bfloat16
I32 = jnp.int32

A_HEADS = 16
A_HEAD_DIM = 64
A_V_DIM = 64
A_Q_RANK = 256
A_KV_RANK = 128
IDX_HEADS = 8
IDX_DIM = 64
IDX_TOPK_MAX = 256
IDX_TOPK_FRAC = 4
B_HEADS = 16
B_HEAD_DIM = 64
REL_BUCKETS = 32
REL_MAX_DIST = 128
TOP_K = 4
SWIGLU_ALPHA = 1.702
SWIGLU_LIMIT = 7.0
LN_EPS = 1e-5
RMS_EPS = 1e-6
DEPTH = 2
DEEPNORM_ALPHA = (2.0 * DEPTH) ** 0.25

LANES = 128
NEG = -1e30
INT_MIN = -2147483648
MASK_KEY = INT_MIN
LOG2E = math.log2(math.e)
VMEM_LIMIT = 56 * 1024 * 1024

TOK_TILE = 512
DSA_T = 128
DSA_KC = 256
DSA_KF = 512
FOX_T = 512
MOE_TM = 512
ROW_TILE = 256


def _dot(a, b):
    return jnp.dot(a, b, preferred_element_type=F32)


def _dot_t(a, b):
    return lax.dot_general(a, b, (((1,), (1,)), ((), ())), preferred_element_type=F32)


def _cparams(sem):
    return pltpu.CompilerParams(dimension_semantics=sem, vmem_limit_bytes=VMEM_LIMIT)


def _layer_norm(v, g, b):
    mu = jnp.mean(v, axis=-1, keepdims=True)
    c = v - mu
    var = jnp.mean(c * c, axis=-1, keepdims=True)
    return c * lax.rsqrt(var + LN_EPS) * g + b


def _rms_norm(v, g):
    return v * lax.rsqrt(jnp.mean(v * v, axis=-1, keepdims=True) + RMS_EPS) * g


def _dsa_proj_kernel(x_ref, wcq_ref, wckv_ref, wki_ref, wwit_ref, gq_ref, gkv_ref, gki_ref, bki_ref,
                     wuq_ref, wuk_ref, wqi_ref,
                     ckv_ref, kidx_ref, widxt_ref, qlat_ref, qidx_ref):
    xb = x_ref[...].astype(BF16)
    cq = _rms_norm(_dot(xb, wcq_ref[...]), gq_ref[...])
    ckv = _rms_norm(_dot(xb, wckv_ref[...]), gkv_ref[...])
    ckv_ref[...] = ckv.astype(BF16)
    kidx = _layer_norm(_dot(xb, wki_ref[...]), gki_ref[...], bki_ref[...])
    kidx_ref[...] = kidx.astype(BF16)
    widxt_ref[...] = _dot_t(wwit_ref[...], xb) * (IDX_HEADS ** -0.5 * IDX_DIM ** -0.5)
    cqb = cq.astype(BF16)
    qb = _dot(cqb, wuq_ref[...]).astype(BF16)
    scale = A_HEAD_DIM ** -0.5 * LOG2E
    for p in range(A_HEADS // 2):
        ql = _dot(qb[:, p * LANES:(p + 1) * LANES], wuk_ref[p]) * scale
        qlat_ref[2 * p] = ql[:, :A_KV_RANK].astype(BF16)
        qlat_ref[2 * p + 1] = ql[:, A_KV_RANK:].astype(BF16)
    for h in range(IDX_HEADS):
        qidx_ref[h] = _dot(cqb, wqi_ref[h]).astype(BF16)


def _dsa_proj(x2, wcq, wckv, wki, wwit, gq, gkv, gki, bki, wuq, wukbd, wqi):
    n, d = x2.shape
    tn = min(TOK_TILE, n)
    full = lambda a: pl.BlockSpec(a.shape, lambda i, _nd=a.ndim: (0,) * _nd)
    weights = (wcq, wckv, wki, wwit, gq, gkv, gki, bki, wuq, wukbd, wqi)
    return pl.pallas_call(
        _dsa_proj_kernel,
        grid=(n // tn,),
        in_specs=[pl.BlockSpec((tn, d), lambda i: (i, 0))] + [full(a) for a in weights],
        out_specs=[
            pl.BlockSpec((tn, A_KV_RANK), lambda i: (i, 0)),
            pl.BlockSpec((tn, IDX_DIM), lambda i: (i, 0)),
            pl.BlockSpec((IDX_HEADS, tn), lambda i: (0, i)),
            pl.BlockSpec((A_HEADS, tn, A_KV_RANK), lambda i: (0, i, 0)),
            pl.BlockSpec((IDX_HEADS, tn, IDX_DIM), lambda i: (0, i, 0)),
        ],
        out_shape=[
            jax.ShapeDtypeStruct((n, A_KV_RANK), BF16),
            jax.ShapeDtypeStruct((n, IDX_DIM), BF16),
            jax.ShapeDtypeStruct((IDX_HEADS, n), F32),
            jax.ShapeDtypeStruct((A_HEADS, n, A_KV_RANK), BF16),
            jax.ShapeDtypeStruct((IDX_HEADS, n, IDX_DIM), BF16),
        ],
        compiler_params=_cparams(("parallel",)),
        name="dsa_proj",
    )(x2, *weights)


def _dsa_attn_kernel(topk, qidx_ref, widxt_ref, kidx_ref, ckv_ref, qlat_ref, bias_ref, wuv_ref, o_ref,
                     keys_scr, jstar_scr, mask_scr, m_scr, l_scr, acc_scr):
    T, KC, KF, H = DSA_T, DSA_KC, DSA_KF, A_HEADS
    SUB = 8
    i = pl.program_id(1)
    t0 = i * T
    n_blk = i + 1
    n_sc = (n_blk * T + KC - 1) // KC
    t_row = t0 + lax.broadcasted_iota(I32, (1, T), 1)
    wt = widxt_ref[...]
    qi = qidx_ref[...].reshape(IDX_HEADS * T, IDX_DIM)

    def score_chunk(c, carry):
        k0 = pl.multiple_of(c * KC, KC)
        s_all = _dot_t(kidx_ref[pl.ds(k0, KC), :], qi)
        acc = jnp.zeros((KC, T), F32)
        for h in range(IDX_HEADS):
            acc = acc + wt[h:h + 1, :] * jnp.maximum(s_all[:, h * T:(h + 1) * T], 0.0)
        acc = jnp.where(acc == 0.0, 0.0, acc)
        bits = pltpu.bitcast(acc, I32)
        key = bits ^ ((bits >> 31) & 0x7FFFFFFF)
        k_abs = k0 + lax.broadcasted_iota(I32, (KC, 1), 0)
        key = jnp.where(k_abs <= t_row, key, MASK_KEY)
        for g in range(KC // LANES):
            keys_scr[c * (KC // LANES) + g] = key[g * LANES:(g + 1) * LANES, :]
        return carry

    lax.fori_loop(0, n_sc, score_chunk, 0)

    def fold(hit):
        return jnp.sum(hit.reshape(LANES // SUB, SUB, T), axis=0)

    def count_ge(cand_s):
        def body(j, part):
            part = part + fold(jnp.where(keys_scr[2 * j] >= cand_s, 1.0, 0.0))
            return part + fold(jnp.where(keys_scr[2 * j + 1] >= cand_s, 1.0, 0.0))

        part = lax.fori_loop(0, n_sc, body, jnp.zeros((SUB, T), F32))
        return jnp.sum(part, axis=0, keepdims=True)

    def bit_body(bi, ans):
        cand = ans | jnp.left_shift(jnp.int32(1), 31 - bi)
        cnt = count_ge(cand ^ INT_MIN)
        return jnp.where(cnt >= topk, cand, ans)

    ans = lax.fori_loop(0, 32, bit_body, jnp.zeros((1, T), I32))
    thr = ans ^ INT_MIN

    def count_gt_eq():
        def body(j, parts):
            kk = keys_scr[j]
            return (parts[0] + fold(jnp.where(kk > thr, 1.0, 0.0)), parts[1] + fold(jnp.where(kk == thr, 1.0, 0.0)))

        z = jnp.zeros((SUB, T), F32)
        pg, pe = lax.fori_loop(0, n_blk, body, (z, z))
        return jnp.sum(pg, axis=0, keepdims=True), jnp.sum(pe, axis=0, keepdims=True)

    cnt_gt, cnt_eq = count_gt_eq()
    need = topk - cnt_gt
    s_total = keys_scr.shape[0] * LANES
    jstar_scr[...] = jnp.full(jstar_scr.shape, s_total, I32)
    tie = jnp.max(jnp.where(cnt_gt + cnt_eq > topk, 1.0, 0.0)) > 0.0

    @pl.when(tie)
    def _():
        n_bits = max(1, (s_total - 1).bit_length())

        def jbit(bi, lo):
            cand = lo + jnp.left_shift(jnp.int32(1), n_bits - 1 - bi)

            def body(j, part):
                pos = j * LANES + lax.broadcasted_iota(I32, (LANES, T), 0)
                hit = jnp.where(keys_scr[j] == thr, 1.0, 0.0)
                return part + fold(jnp.where(pos < cand, hit, 0.0))

            part = lax.fori_loop(0, n_blk, body, jnp.zeros((SUB, T), F32))
            c = jnp.sum(part, axis=0, keepdims=True)
            return jnp.where(c < need, cand, lo)

        lo = lax.fori_loop(0, n_bits, jbit, jnp.zeros((1, T), I32))
        jstar_scr[...] = jnp.broadcast_to(lo, jstar_scr.shape)

    jstar = jstar_scr[0:1, :]

    def mask_chunk(c, carry):
        for g in range(KC // LANES):
            j = c * (KC // LANES) + g
            kk = keys_scr[j]
            k_abs = j * LANES + lax.broadcasted_iota(I32, (LANES, 1), 0)
            at_thr = jnp.where(kk == thr, jnp.where(k_abs <= jstar, 0.0, NEG), NEG)
            madd = jnp.where(kk > thr, 0.0, at_thr)
            mask_scr[j] = jnp.where(k_abs <= t_row, madd, NEG).T
        return carry

    lax.fori_loop(0, n_sc, mask_chunk, 0)

    m_scr[...] = jnp.full(m_scr.shape, NEG, F32)
    l_scr[...] = jnp.zeros(l_scr.shape, F32)
    acc_scr[...] = jnp.zeros(acc_scr.shape, F32)
    q = qlat_ref[...].reshape(H * T, A_KV_RANK)

    def attend(blk0, k0, kw, limit, bias):
        kv = ckv_ref[pl.ds(k0, kw), :]
        s = _dot_t(q, kv)
        madd = jnp.concatenate([mask_scr[blk0 + g] for g in range(kw // LANES)], axis=1)
        if limit is not None:
            s_abs = k0 + lax.broadcasted_iota(I32, (1, kw), 1)
            madd = jnp.where(s_abs < limit, madd, NEG)
        s3 = s.reshape(H, T, kw) + madd[None]
        if bias is not None:
            s3 = s3 + bias
        s = s3.reshape(H * T, kw)
        m_prev = m_scr[...]
        m_new = jnp.maximum(m_prev, jnp.max(s, axis=1, keepdims=True))
        alpha = jnp.exp2(m_prev - m_new)
        p = jnp.exp2(s - jnp.concatenate([m_new] * (kw // LANES), axis=1))
        l_scr[...] = alpha * l_scr[...] + jnp.sum(p, axis=1, keepdims=True)
        acc_scr[...] = alpha * acc_scr[...] + _dot(p.astype(BF16), kv)
        m_scr[...] = m_new

    far_end = jnp.maximum(t0 - T, 0)
    n_far = (far_end + KF - 1) // KF

    def far_chunk(c, carry):
        k0 = pl.multiple_of(c * KF, KF)
        attend(c * (KF // LANES), k0, KF, far_end, None)
        return carry

    lax.fori_loop(0, n_far, far_chunk, 0)
    near0 = pl.multiple_of(far_end, LANES)
    attend(far_end // LANES, near0, KC, None, bias_ref[0])

    o = (acc_scr[...] / l_scr[...]).astype(BF16).reshape(H, T, A_KV_RANK)
    for p in range(H // 2):
        pair = jnp.concatenate([o[2 * p], o[2 * p + 1]], axis=1)
        o_ref[:, p * LANES:(p + 1) * LANES] = _dot(pair, wuv_ref[p]).astype(BF16)


def _dsa_attn(bsz, seq, qidx, widx, kidx, ckv, qlat, bias, wuvbd):
    T = DSA_T
    nq = seq // T
    n = bsz * seq
    topk = min(IDX_TOPK_MAX, seq // IDX_TOPK_FRAC)
    kernel = functools.partial(_dsa_attn_kernel, float(topk))
    return pl.pallas_call(
        kernel,
        grid=(bsz, nq),
        in_specs=[
            pl.BlockSpec((IDX_HEADS, T, IDX_DIM), lambda b, i: (0, b * nq + i, 0)),
            pl.BlockSpec((IDX_HEADS, T), lambda b, i: (0, b * nq + i)),
            pl.BlockSpec((seq, IDX_DIM), lambda b, i: (b, 0)),
            pl.BlockSpec((seq, A_KV_RANK), lambda b, i: (b, 0)),
            pl.BlockSpec((A_HEADS, T, A_KV_RANK), lambda b, i: (0, b * nq + i, 0)),
            pl.BlockSpec((1, A_HEADS, T, DSA_KC), lambda b, i: (jnp.minimum(i, 1), 0, 0, 0)),
            pl.BlockSpec(wuvbd.shape, lambda b, i: (0, 0, 0)),
        ],
        out_specs=pl.BlockSpec((T, A_HEADS * A_V_DIM), lambda b, i: (b * nq + i, 0)),
        out_shape=jax.ShapeDtypeStruct((n, A_HEADS * A_V_DIM), BF16),
        scratch_shapes=[
            pltpu.VMEM((seq // LANES, LANES, T), I32),
            pltpu.VMEM((8, T), I32),
            pltpu.VMEM((seq // LANES, T, LANES), F32),
            pltpu.VMEM((A_HEADS * T, LANES), F32),
            pltpu.VMEM((A_HEADS * T, LANES), F32),
            pltpu.VMEM((A_HEADS * T, A_KV_RANK), F32),
        ],
        compiler_params=_cparams(("parallel", "arbitrary")),
        name="dsa_attn",
    )(qidx, widx, kidx, ckv, qlat, bias, wuvbd)


def _t5_bucket_np(dist):
    max_exact = REL_BUCKETS // 2
    n = np.maximum(dist, 0)
    ratio = np.log(np.maximum(n, 1).astype(np.float32) / np.float32(max_exact)) / np.float32(math.log(REL_MAX_DIST / max_exact))
    large = max_exact + (ratio * np.float32(REL_BUCKETS - max_exact)).astype(np.int32)
    large = np.minimum(large, REL_BUCKETS - 1)
    return np.where(n < max_exact, n, large)


def _dsa_bias_tiles(rel_table):
    T, KC = DSA_T, DSA_KC
    t = np.arange(T)[:, None]
    c = np.arange(KC)[None, :]
    d_first = t - c
    d_rest = t + T - c
    idx = np.stack([_t5_bucket_np(d_first), _t5_bucket_np(d_rest)]).astype(np.int32)
    far_bucket = int(_t5_bucket_np(np.array([T + 1]))[0])
    assert far_bucket == REL_BUCKETS - 1 and int(_t5_bucket_np(np.array([T - 15]))[0]) == far_bucket
    tiles = (rel_table[jnp.asarray(idx)] - rel_table[far_bucket]) * LOG2E
    return jnp.transpose(tiles, (0, 3, 1, 2)).astype(F32)


def _block_diag_pairs(w):
    h, a, b = w.shape
    z = jnp.zeros((h // 2, a, b), w.dtype)
    top = jnp.concatenate([w[0::2], z], axis=2)
    bot = jnp.concatenate([z, w[1::2]], axis=2)
    return jnp.concatenate([top, bot], axis=1)


def _dsa_mixer(x2, bsz, seq, w_in, g_q, g_kv, g_kidx, b_kidx, w_uq, w_uk, w_uv, w_qidx, rel_table):
    d = x2.shape[1]
    wcq = w_in[:, :A_Q_RANK].astype(BF16)
    wckv = w_in[:, A_Q_RANK:A_Q_RANK + A_KV_RANK].astype(BF16)
    wki = w_in[:, A_Q_RANK + A_KV_RANK:A_Q_RANK + A_KV_RANK + IDX_DIM].astype(BF16)
    wwit = jnp.transpose(w_in[:, A_Q_RANK + A_KV_RANK + IDX_DIM:]).astype(BF16)
    wuq = w_uq.reshape(A_Q_RANK, A_HEADS * A_HEAD_DIM).astype(BF16)
    wukbd = _block_diag_pairs(jnp.transpose(w_uk, (1, 2, 0))).astype(BF16)
    wuvbd = _block_diag_pairs(jnp.transpose(w_uv, (1, 0, 2))).astype(BF16)
    wqi = jnp.transpose(w_qidx, (1, 0, 2)).astype(BF16)
    row = lambda v: v.reshape(1, -1).astype(F32)
    ckv, kidx, widx, qlat, qidx = _dsa_proj(x2, wcq, wckv, wki, wwit, row(g_q), row(g_kv), row(g_kidx), row(b_kidx),
                                            wuq, wukbd, wqi)
    bias = _dsa_bias_tiles(rel_table)
    return _dsa_attn(bsz, seq, qidx, widx, kidx, ckv, qlat, bias, wuvbd)


def _fox_proj_kernel(tiles_per_seq, x_ref, wq_ref, wk_ref, wv_ref, wft_ref, bf_ref,
                     q_ref, k_ref, v_ref, cum_ref, carry_scr):
    i = pl.program_id(0)
    tn = x_ref.shape[0]

    @pl.when(i % tiles_per_seq == 0)
    def _():
        carry_scr[...] = jnp.zeros(carry_scr.shape, F32)

    xb = x_ref[...].astype(BF16)
    q_ref[...] = (_dot(xb, wq_ref[...]) * (B_HEAD_DIM ** -0.5 * LOG2E)).astype(BF16)
    k_ref[...] = _dot(xb, wk_ref[...]).astype(BF16)
    v_ref[...] = _dot(xb, wv_ref[...]).astype(BF16)
    z = _dot_t(wft_ref[...], xb) + bf_ref[...]
    logf = jnp.minimum(z, 0.0) - jnp.log(1.0 + jnp.exp(-jnp.abs(z)))
    r = lax.broadcasted_iota(I32, (tn, tn), 0)
    c = lax.broadcasted_iota(I32, (tn, tn), 1)
    tri = jnp.where(r <= c, 1.0, 0.0).astype(BF16)
    hi = logf.astype(BF16)
    lo = (logf - hi.astype(F32)).astype(BF16)
    cum = _dot(hi, tri) + _dot(lo, tri) + carry_scr[:, 0:1]
    cum_ref[...] = cum * LOG2E
    carry_scr[...] = jnp.broadcast_to(cum[:, tn - 1:tn], carry_scr.shape)


def _fox_proj(x2, seq, wq, wk, wv, wft, bf):
    n, d = x2.shape
    tn = min(TOK_TILE, seq)
    hd = B_HEADS * B_HEAD_DIM
    full = lambda a: pl.BlockSpec(a.shape, lambda i, _nd=a.ndim: (0,) * _nd)
    kernel = functools.partial(_fox_proj_kernel, seq // tn)
    return pl.pallas_call(
        kernel,
        grid=(n // tn,),
        in_specs=[pl.BlockSpec((tn, d), lambda i: (i, 0))] + [full(a) for a in (wq, wk, wv, wft, bf)],
        out_specs=[pl.BlockSpec((tn, hd), lambda i: (i, 0))] * 3 + [pl.BlockSpec((B_HEADS, tn), lambda i: (0, i))],
        out_shape=[jax.ShapeDtypeStruct((n, hd), BF16)] * 3 + [jax.ShapeDtypeStruct((B_HEADS, n), F32)],
        scratch_shapes=[pltpu.VMEM((B_HEADS, LANES), F32)],
        compiler_params=_cparams(("arbitrary",)),
        name="fox_proj",
    )(x2, wq, wk, wv, wft, bf)


def _fox_attn_kernel(q_ref, k_ref, v_ref, cum_ref, o_ref, m_scr, l_scr, acc_scr):
    T = FOX_T
    i = pl.program_id(2)
    lane = lax.broadcasted_iota(I32, (1, LANES), 1)
    lo_half = lane < B_HEAD_DIM
    q = q_ref[...]
    zero = jnp.zeros_like(q)
    q2 = (jnp.where(lo_half, q, zero), jnp.where(lo_half, zero, q))
    m_scr[...] = jnp.full(m_scr.shape, NEG, F32)
    l_scr[...] = jnp.zeros(l_scr.shape, F32)
    acc_scr[...] = jnp.zeros(acc_scr.shape, F32)

    def attend(k0, causal):
        kc = k_ref[pl.ds(k0, T), :]
        vc = v_ref[pl.ds(k0, T), :]
        cs = cum_ref[0, k0 // T]
        pv = []
        alphas = []
        for h in range(2):
            s = _dot_t(q2[h], kc) - cs[h:h + 1, :]
            if causal:
                r = lax.broadcasted_iota(I32, (T, T), 0)
                c = lax.broadcasted_iota(I32, (T, T), 1)
                s = jnp.where(c <= r, s, NEG)
            m_prev = m_scr[h]
            m_new = jnp.maximum(m_prev, jnp.max(s, axis=1, keepdims=True))
            alpha = jnp.exp2(m_prev - m_new)
            p = jnp.exp2(s - jnp.concatenate([m_new] * (T // LANES), axis=1))
            l_scr[h] = alpha * l_scr[h] + jnp.sum(p, axis=1, keepdims=True)
            m_scr[h] = m_new
            pv.append(_dot(p.astype(BF16), vc))
            alphas.append(alpha)
        acc_scr[...] = jnp.where(lo_half, alphas[0], alphas[1]) * acc_scr[...] + jnp.where(lo_half, pv[0], pv[1])

    def far(c, carry):
        attend(pl.multiple_of(c * T, T), False)
        return carry

    lax.fori_loop(0, i, far, 0)
    attend(pl.multiple_of(i * T, T), True)
    o_ref[...] = (acc_scr[...] / jnp.where(lo_half, l_scr[0], l_scr[1])).astype(BF16)


def _fox_attn(bsz, seq, q, k, v, cum3):
    T = FOX_T
    nq = seq // T
    n = bsz * seq
    pairs = B_HEADS // 2
    return pl.pallas_call(
        _fox_attn_kernel,
        grid=(bsz, pairs, nq),
        in_specs=[
            pl.BlockSpec((T, LANES), lambda b, p, i: (b * nq + i, p)),
            pl.BlockSpec((seq, LANES), lambda b, p, i: (b, p)),
            pl.BlockSpec((seq, LANES), lambda b, p, i: (b, p)),
            pl.BlockSpec((1, nq, 2, T), lambda b, p, i: (p, b, 0, 0)),
        ],
        out_specs=pl.BlockSpec((T, LANES), lambda b, p, i: (b * nq + i, p)),
        out_shape=jax.ShapeDtypeStruct((n, B_HEADS * B_HEAD_DIM), BF16),
        scratch_shapes=[
            pltpu.VMEM((2, T, LANES), F32),
            pltpu.VMEM((2, T, LANES), F32),
            pltpu.VMEM((T, LANES), F32),
        ],
        compiler_params=_cparams(("parallel", "parallel", "arbitrary")),
        name="fox_attn",
    )(q, k, v, cum3)


def _fox_mixer(x2, bsz, seq, w_in, b_f):
    hd = B_HEADS * B_HEAD_DIM
    wq = w_in[:, :hd].astype(BF16)
    wk = w_in[:, hd:2 * hd].astype(BF16)
    wv = w_in[:, 2 * hd:3 * hd].astype(BF16)
    wft = jnp.transpose(w_in[:, 3 * hd:]).astype(BF16)
    bf = b_f.reshape(B_HEADS, 1).astype(F32)
    q, k, v, cum = _fox_proj(x2, seq, wq, wk, wv, wft, bf)
    cum3 = jnp.transpose(cum.reshape(B_HEADS // 2, 2, bsz * seq // FOX_T, FOX_T), (0, 2, 1, 3))
    return _fox_attn(bsz, seq, q, k, v, cum3)


def _oproj_route_kernel(a_ref, x_ref, wo_ref, g_ref, b_ref, wr_ref, br_ref,
                        x1_ref, tope_ref, rank_ref, gate_ref, cnt_ref, carry_scr):
    i = pl.program_id(0)
    tn = x_ref.shape[0]
    ne = wr_ref.shape[1]

    @pl.when(i == 0)
    def _():
        carry_scr[...] = jnp.zeros(carry_scr.shape, F32)

    h = _dot(a_ref[...], wo_ref[...])
    x1 = _layer_norm(DEEPNORM_ALPHA * x_ref[...] + h, g_ref[...], b_ref[...])
    x1_ref[...] = x1
    logits = _dot(x1.astype(BF16), wr_ref[...]) + br_ref[...]
    lane = lax.broadcasted_iota(I32, (tn, ne), 1)
    work = logits
    vals, hots = [], []
    for k in range(TOP_K):
        m = jnp.max(work, axis=1, keepdims=True)
        idx = jnp.min(jnp.where(work == m, lane, ne), axis=1, keepdims=True)
        hot = lane == idx
        vals.append(m)
        hots.append(hot)
        tope_ref[:, k:k + 1] = idx
        work = jnp.where(hot, -jnp.inf, work)
    es = [jnp.exp(v - vals[0]) for v in vals]
    den = es[0] + es[1] + es[2] + es[3]
    for k in range(TOP_K):
        gate_ref[:, k:k + 1] = es[k] / den
    sel = jnp.zeros((tn, ne), F32)
    for hot in hots:
        sel = sel + jnp.where(hot, 1.0, 0.0)
    r = lax.broadcasted_iota(I32, (tn, tn), 0)
    c = lax.broadcasted_iota(I32, (tn, tn), 1)
    tril = jnp.where(c < r, 1.0, 0.0).astype(BF16)
    prefix = _dot(tril, sel.astype(BF16)) + carry_scr[0:1, :]
    for k in range(TOP_K):
        rank_ref[:, k:k + 1] = jnp.sum(jnp.where(hots[k], prefix, 0.0), axis=1, keepdims=True).astype(I32)
    total = carry_scr[0:1, :] + jnp.sum(sel, axis=0, keepdims=True)
    carry_scr[...] = jnp.broadcast_to(total, carry_scr.shape)
    cnt_ref[...] = total.astype(I32)


def _oproj_route(attn, x2, wo, g, b, wr, br):
    n, d = x2.shape
    tn = min(TOK_TILE, n)
    ne = wr.shape[1]
    full = lambda a: pl.BlockSpec(a.shape, lambda i, _nd=a.ndim: (0,) * _nd)
    tok = lambda w: pl.BlockSpec((tn, w), lambda i: (i, 0))
    return pl.pallas_call(
        _oproj_route_kernel,
        grid=(n // tn,),
        in_specs=[tok(attn.shape[1]), tok(d)] + [full(a) for a in (wo, g, b, wr, br)],
        out_specs=[tok(d), tok(TOP_K), tok(TOP_K), tok(TOP_K), pl.BlockSpec((1, ne), lambda i: (0, 0))],
        out_shape=[
            jax.ShapeDtypeStruct((n, d), F32),
            jax.ShapeDtypeStruct((n, TOP_K), I32),
            jax.ShapeDtypeStruct((n, TOP_K), I32),
            jax.ShapeDtypeStruct((n, TOP_K), F32),
            jax.ShapeDtypeStruct((1, ne), I32),
        ],
        scratch_shapes=[pltpu.VMEM((8, ne), F32)],
        compiler_params=_cparams(("arbitrary",)),
        name="oproj_route",
    )(attn, x2, wo, g, b, wr, br)


def _dispatch_kernel(dest_ref, x_ref, xs_in_hbm, xs_hbm, sem):
    del xs_in_hbm
    tn = x_ref.shape[0]

    def issue(t, carry):
        for k in range(TOP_K):
            d = dest_ref[t * TOP_K + k]
            pltpu.make_async_copy(x_ref.at[pl.ds(t, 1)], xs_hbm.at[pl.ds(d, 1)], sem).start()
        return carry

    lax.fori_loop(0, tn, issue, 0)
    for k in range(TOP_K):
        pltpu.make_async_copy(x_ref, xs_hbm.at[pl.ds(0, tn)], sem).wait()


def _dispatch(dest_flat, x1, n_slots):
    n, d = x1.shape
    tn = min(ROW_TILE, n)
    xs0 = jnp.zeros((n_slots, d), x1.dtype)
    return pl.pallas_call(
        _dispatch_kernel,
        grid=(n // tn,),
        in_specs=[
            pl.BlockSpec((tn * TOP_K,), lambda i: (i,), memory_space=pltpu.SMEM),
            pl.BlockSpec((tn, d), lambda i: (i, 0)),
            pl.BlockSpec(memory_space=pl.ANY),
        ],
        out_specs=pl.BlockSpec(memory_space=pl.ANY),
        out_shape=jax.ShapeDtypeStruct((n_slots, d), x1.dtype),
        scratch_shapes=[pltpu.SemaphoreType.DMA(())],
        input_output_aliases={2: 0},
        compiler_params=pltpu.CompilerParams(dimension_semantics=("arbitrary",), has_side_effects=True),
        name="moe_dispatch",
    )(dest_flat, x1, xs0)


def _expert_kernel(te_ref, tv_ref, xs_ref, wgu_ref, bgu_ref, wd_ref, bd_ref, y_ref, wgu_b, wd_b):
    t = pl.program_id(0)
    f = wd_ref.shape[1]

    @pl.when((t == 0) | (te_ref[t] != te_ref[jnp.maximum(t - 1, 0)]))
    def _():
        wgu_b[...] = wgu_ref[0].astype(BF16)
        wd_b[...] = wd_ref[0].astype(BF16)

    @pl.when(tv_ref[t] > 0)
    def _():
        xb = xs_ref[...].astype(BF16)
        h = _dot(xb, wgu_b[...]) + bgu_ref[0]
        gate = jnp.minimum(h[:, :f], SWIGLU_LIMIT)
        up = jnp.clip(h[:, f:], -SWIGLU_LIMIT, SWIGLU_LIMIT)
        act = (up + 1.0) * (gate * (1.0 / (1.0 + jnp.exp(-SWIGLU_ALPHA * gate))))
        y_ref[...] = _dot(act.astype(BF16), wd_b[...]) + bd_ref[0]

    @pl.when(tv_ref[t] == 0)
    def _():
        y_ref[...] = jnp.zeros(y_ref.shape, F32)


def _experts(layer, tile_e, tile_v, xs, wgu, bgu, wd, bd):
    n_slots, d = xs.shape
    tm = MOE_TM
    f2 = wgu.shape[3]
    f = wd.shape[2]
    grid_spec = pltpu.PrefetchScalarGridSpec(
        num_scalar_prefetch=2,
        grid=(n_slots // tm,),
        in_specs=[
            pl.BlockSpec((tm, d), lambda t, te, tv: (t, 0)),
            pl.BlockSpec((None, 1, d, f2), lambda t, te, tv: (layer, te[t], 0, 0)),
            pl.BlockSpec((1, 1, f2), lambda t, te, tv: (te[t], 0, 0)),
            pl.BlockSpec((None, 1, f, d), lambda t, te, tv: (layer, te[t], 0, 0)),
            pl.BlockSpec((1, 1, d), lambda t, te, tv: (te[t], 0, 0)),
        ],
        out_specs=pl.BlockSpec((tm, d), lambda t, te, tv: (t, 0)),
        scratch_shapes=[pltpu.VMEM((d, f2), BF16), pltpu.VMEM((f, d), BF16)],
    )
    return pl.pallas_call(
        _expert_kernel,
        grid_spec=grid_spec,
        out_shape=jax.ShapeDtypeStruct((n_slots, d), F32),
        compiler_params=_cparams(("arbitrary",)),
        name="moe_experts",
    )(tile_e, tile_v, xs, wgu, bgu, wd, bd)


def _combine_kernel(dest_ref, dest_next_ref, y_hbm, gate_ref, x_ref, g_ref, b_ref, o_ref, ybuf, sem):
    tn = x_ref.shape[0]
    i = pl.program_id(0)
    slot = i % 2

    def issue(dref, s):
        def body(t, carry):
            for k in range(TOP_K):
                d = dref[t * TOP_K + k]
                pltpu.make_async_copy(y_hbm.at[pl.ds(d, 1)], ybuf.at[s, k, pl.ds(t, 1)], sem.at[s]).start()
            return carry

        lax.fori_loop(0, tn, body, 0)

    @pl.when(i == 0)
    def _():
        issue(dest_ref, 0)

    @pl.when(i + 1 < pl.num_programs(0))
    def _():
        issue(dest_next_ref, 1 - slot)

    for k in range(TOP_K):
        pltpu.make_async_copy(y_hbm.at[pl.ds(0, tn)], ybuf.at[slot, k], sem.at[slot]).wait()
    gates = gate_ref[...]
    fsum = gates[:, 0:1] * ybuf[slot, 0]
    for k in range(1, TOP_K):
        fsum = fsum + gates[:, k:k + 1] * ybuf[slot, k]
    o_ref[...] = _layer_norm(DEEPNORM_ALPHA * x_ref[...] + fsum, g_ref[...], b_ref[...])


def _combine(dest_flat, y, gates, x1, g, b):
    n, d = x1.shape
    tn = min(ROW_TILE, n)
    full = lambda a: pl.BlockSpec(a.shape, lambda i, _nd=a.ndim: (0,) * _nd)
    last = n // tn - 1
    return pl.pallas_call(
        _combine_kernel,
        grid=(n // tn,),
        in_specs=[
            pl.BlockSpec((tn * TOP_K,), lambda i: (i,), memory_space=pltpu.SMEM),
            pl.BlockSpec((tn * TOP_K,), lambda i: (jnp.minimum(i + 1, last),), memory_space=pltpu.SMEM),
            pl.BlockSpec(memory_space=pl.ANY),
            pl.BlockSpec((tn, TOP_K), lambda i: (i, 0)),
            pl.BlockSpec((tn, d), lambda i: (i, 0)),
            full(g), full(b),
        ],
        out_specs=pl.BlockSpec((tn, d), lambda i: (i, 0)),
        out_shape=jax.ShapeDtypeStruct((n, d), F32),
        scratch_shapes=[pltpu.VMEM((2, TOP_K, tn, d), F32), pltpu.SemaphoreType.DMA((2,))],
        compiler_params=_cparams(("arbitrary",)),
        name="moe_combine",
    )(dest_flat, dest_flat, y, gates, x1, g, b)


def _moe(layer, x1, top_e, rank, gates, counts, w_gu, b_gu, w_down, b_down, ln_g, ln_b):
    n, d = x1.shape
    ne = w_gu.shape[1]
    tm = MOE_TM
    n_pairs = n * TOP_K
    n_slots = -(-n_pairs // tm) * tm + ne * tm
    n_tiles = n_slots // tm
    counts = counts.reshape(ne)
    padded = (counts + tm - 1) // tm * tm
    eidx = jnp.arange(ne, dtype=I32)
    pad_ends = jnp.sum(jnp.where(eidx[None, :] <= eidx[:, None], padded[None, :], 0), axis=1)
    pad_starts = pad_ends - padded
    start_of_pair = jnp.sum(jnp.where(top_e[..., None] == eidx, pad_starts, 0), axis=-1)
    dest = (start_of_pair + rank).reshape(n_pairs).astype(I32)
    tile_start = jnp.arange(n_tiles, dtype=I32) * tm
    tile_v = (tile_start < pad_ends[ne - 1]).astype(I32)
    tile_e = jnp.minimum(jnp.sum((pad_ends[None, :] <= tile_start[:, None]).astype(I32), axis=1), ne - 1)
    xs = _dispatch(dest, x1, n_slots)
    y = _experts(layer, tile_e, tile_v, xs, w_gu, b_gu.reshape(ne, 1, -1).astype(F32),
                 w_down, b_down.reshape(ne, 1, -1).astype(F32))
    return _combine(dest, y, gates, x1, ln_g, ln_b)


def kernel(x, rel_table, a_w_in, a_g_q, a_g_kv, a_g_kidx, a_b_kidx, a_w_uq, a_w_uk, a_w_uv, a_w_qidx, a_w_o,
           b_w_in, b_b_f, b_w_o, ln_mix_g, ln_mix_b, ln_ffn_g, ln_ffn_b, w_router, b_router, w_gu, b_gu,
           w_down, b_down):
    bsz, seq, d = x.shape
    n_mixers = 2
    depth = ln_mix_g.shape[0]
    x2 = x.reshape(bsz * seq, d)
    row = lambda v: v.reshape(1, -1).astype(F32)
    for i in range(depth):
        j = i // n_mixers
        if i % n_mixers == 0:
            attn = _dsa_mixer(x2, bsz, seq, a_w_in[j], a_g_q[j], a_g_kv[j], a_g_kidx[j], a_b_kidx[j],
                              a_w_uq[j], a_w_uk[j], a_w_uv[j], a_w_qidx[j], rel_table)
            wo = a_w_o[j]
        else:
            attn = _fox_mixer(x2, bsz, seq, b_w_in[j], b_b_f[j])
            wo = b_w_o[j]
        x1, top_e, rank, gates, counts = _oproj_route(attn, x2, wo.astype(BF16), row(ln_mix_g[i]), row(ln_mix_b[i]),
                                                      w_router[i].astype(BF16), row(b_router[i]))
        x2 = _moe(i, x1, top_e, rank, gates, counts, w_gu, b_gu[i], w_down, b_down[i],
                  row(ln_ffn_g[i]), row(ln_ffn_b[i]))
    return x2.reshape(bsz, seq, d)
```

```python
import functools
import math

import numpy as np
import jax
import jax.numpy as jnp
from jax import lax
from jax.experimental import pallas as pl
from jax.experimental.pallas import tpu as pltpu

F32 = jnp.float32
BF16 = jnp.bfloat16
I32 = jnp.int32

A_HEADS = 16
A_HEAD_DIM = 64
A_V_DIM = 64
A_Q_RANK = 256
A_KV_RANK = 128
IDX_HEADS = 8
IDX_DIM = 64
IDX_TOPK_MAX = 256
IDX_TOPK_FRAC = 4
B_HEADS = 16
B_HEAD_DIM = 64
REL_BUCKETS = 32
REL_MAX_DIST = 128
TOP_K = 4
SWIGLU_ALPHA = 1.702
SWIGLU_LIMIT = 7.0
LN_EPS = 1e-5
RMS_EPS = 1e-6
DEPTH = 2
DEEPNORM_ALPHA = (2.0 * DEPTH) ** 0.25

LANES = 128
NEG = -1e30
INT_MIN = -2147483648
MASK_KEY = INT_MIN
LOG2E = math.log2(math.e)
VMEM_LIMIT = 56 * 1024 * 1024

TOK_TILE = 512
DSA_T = 128
DSA_KC = 256
DSA_KF = 512
FOX_T = 512
MOE_TM = 512
ROW_TILE = 256


def _dot(a, b):
    return jnp.dot(a, b, preferred_element_type=F32)


def _dot_t(a, b):
    return lax.dot_general(a, b, (((1,), (1,)), ((), ())), preferred_element_type=F32)


def _cparams(sem):
    return pltpu.CompilerParams(dimension_semantics=sem, vmem_limit_bytes=VMEM_LIMIT)


def _layer_norm(v, g, b):
    mu = jnp.mean(v, axis=-1, keepdims=True)
    c = v - mu
    var = jnp.mean(c * c, axis=-1, keepdims=True)
    return c * lax.rsqrt(var + LN_EPS) * g + b


def _rms_norm(v, g):
    return v * lax.rsqrt(jnp.mean(v * v, axis=-1, keepdims=True) + RMS_EPS) * g


def _dsa_proj_kernel(x_ref, wcq_ref, wckv_ref, wki_ref, wwit_ref, gq_ref, gkv_ref, gki_ref, bki_ref,
                     wuq_ref, wuk_ref, wqi_ref,
                     ckv_ref, kidx_ref, widxt_ref, qlat_ref, qidx_ref):
    xb = x_ref[...].astype(BF16)
    cq = _rms_norm(_dot(xb, wcq_ref[...]), gq_ref[...])
    ckv = _rms_norm(_dot(xb, wckv_ref[...]), gkv_ref[...])
    ckv_ref[...] = ckv.astype(BF16)
    kidx = _layer_norm(_dot(xb, wki_ref[...]), gki_ref[...], bki_ref[...])
    kidx_ref[...] = kidx.astype(BF16)
    widxt_ref[...] = _dot_t(wwit_ref[...], xb) * (IDX_HEADS ** -0.5 * IDX_DIM ** -0.5)
    cqb = cq.astype(BF16)
    qb = _dot(cqb, wuq_ref[...]).astype(BF16)
    scale = A_HEAD_DIM ** -0.5 * LOG2E
    for p in range(A_HEADS // 2):
        ql = _dot(qb[:, p * LANES:(p + 1) * LANES], wuk_ref[p]) * scale
        qlat_ref[2 * p] = ql[:, :A_KV_RANK].astype(BF16)
        qlat_ref[2 * p + 1] = ql[:, A_KV_RANK:].astype(BF16)
    for h in range(IDX_HEADS):
        qidx_ref[h] = _dot(cqb, wqi_ref[h]).astype(BF16)


def _dsa_proj(x2, wcq, wckv, wki, wwit, gq, gkv, gki, bki, wuq, wukbd, wqi):
    n, d = x2.shape
    tn = min(TOK_TILE, n)
    full = lambda a: pl.BlockSpec(a.shape, lambda i, _nd=a.ndim: (0,) * _nd)
    weights = (wcq, wckv, wki, wwit, gq, gkv, gki, bki, wuq, wukbd, wqi)
    return pl.pallas_call(
        _dsa_proj_kernel,
        grid=(n // tn,),
        in_specs=[pl.BlockSpec((tn, d), lambda i: (i, 0))] + [full(a) for a in weights],
        out_specs=[
            pl.BlockSpec((tn, A_KV_RANK), lambda i: (i, 0)),
            pl.BlockSpec((tn, IDX_DIM), lambda i: (i, 0)),
            pl.BlockSpec((IDX_HEADS, tn), lambda i: (0, i)),
            pl.BlockSpec((A_HEADS, tn, A_KV_RANK), lambda i: (0, i, 0)),
            pl.BlockSpec((IDX_HEADS, tn, IDX_DIM), lambda i: (0, i, 0)),
        ],
        out_shape=[
            jax.ShapeDtypeStruct((n, A_KV_RANK), BF16),
            jax.ShapeDtypeStruct((n, IDX_DIM), BF16),
            jax.ShapeDtypeStruct((IDX_HEADS, n), F32),
            jax.ShapeDtypeStruct((A_HEADS, n, A_KV_RANK), BF16),
            jax.ShapeDtypeStruct((IDX_HEADS, n, IDX_DIM), BF16),
        ],
        compiler_params=_cparams(("parallel",)),
        name="dsa_proj",
    )(x2, *weights)


def _dsa_attn_kernel(topk, qidx_ref, widxt_ref, kidx_ref, ckv_ref, qlat_ref, bias_ref, wuv_ref, o_ref,
                     keys_scr, hi_scr, lo_scr, jstar_scr, mask_scr, m_scr, l_scr, acc_scr):
    T, KC, KF, H = DSA_T, DSA_KC, DSA_KF, A_HEADS
    SUB = 8
    i = pl.program_id(1)
    t0 = i * T
    n_blk = i + 1
    n_sc = (n_blk * T + KC - 1) // KC
    t_row = t0 + lax.broadcasted_iota(I32, (1, T), 1)
    wt = widxt_ref[...]
    qi = qidx_ref[...].reshape(IDX_HEADS * T, IDX_DIM)

    def score_chunk(c, carry):
        k0 = pl.multiple_of(c * KC, KC)
        s_all = _dot_t(kidx_ref[pl.ds(k0, KC), :], qi)
        acc = jnp.zeros((KC, T), F32)
        for h in range(IDX_HEADS):
            acc = acc + wt[h:h + 1, :] * jnp.maximum(s_all[:, h * T:(h + 1) * T], 0.0)
        acc = jnp.where(acc == 0.0, 0.0, acc)
        bits = pltpu.bitcast(acc, I32)
        key = bits ^ ((bits >> 31) & 0x7FFFFFFF)
        k_abs = k0 + lax.broadcasted_iota(I32, (KC, 1), 0)
        key = jnp.where(k_abs <= t_row, key, MASK_KEY)
        for g in range(KC // LANES):
            blk = key[g * LANES:(g + 1) * LANES, :]
            keys_scr[c * (KC // LANES) + g] = blk
            hi_scr[c * (KC // LANES) + g] = (blk >> 16).astype(jnp.int16)
        return carry

    lax.fori_loop(0, n_sc, score_chunk, 0)

    def fold(hit):
        return jnp.sum(hit.reshape(LANES // SUB, SUB, T), axis=0)

    PACK = 2 * SUB
    one16, zero16 = jnp.int16(1), jnp.int16(0)

    def fold16(hit):
        parts = [hit[r * PACK:(r + 1) * PACK, :] for r in range(LANES // PACK)]
        while len(parts) > 1:
            parts = [a + b for a, b in zip(parts[0::2], parts[1::2])]
        return parts[0]

    def search16(half_scr, base):
        def count_ge(cand_s):
            def body(j, part):
                part = part + fold16(jnp.where(half_scr[2 * j] >= cand_s, one16, zero16))
                return part + fold16(jnp.where(half_scr[2 * j + 1] >= cand_s, one16, zero16))

            part = lax.fori_loop(0, n_sc, body, jnp.zeros((PACK, T), jnp.int16))
            return jnp.sum(part.astype(F32), axis=0, keepdims=True)

        def bit_body(bi, ans):
            cand = ans | jnp.left_shift(jnp.int32(1), 15 - bi)
            cnt = base + count_ge((cand ^ 0x8000).astype(jnp.int16))
            return jnp.where(cnt >= topk, cand, ans)

        return lax.fori_loop(0, 16, bit_body, jnp.zeros((1, T), I32))

    ans_hi = search16(hi_scr, 0.0)
    thr_hi = (ans_hi ^ 0x8000).astype(jnp.int16)

    def low_halves(j, part):
        hi = hi_scr[j]
        lo = ((keys_scr[j] & 0xFFFF) ^ 0x8000).astype(jnp.int16)
        lo_scr[j] = jnp.where(hi == thr_hi, lo, jnp.int16(-32768))
        return part + fold16(jnp.where(hi > thr_hi, one16, zero16))

    part_hi = lax.fori_loop(0, n_sc * (KC // LANES), low_halves, jnp.zeros((PACK, T), jnp.int16))
    cnt_hi = jnp.sum(part_hi.astype(F32), axis=0, keepdims=True)
    ans_lo = search16(lo_scr, cnt_hi)
    thr = jnp.left_shift(ans_hi ^ 0x8000, 16) | ans_lo

    def count_gt_eq():
        def body(j, parts):
            kk = keys_scr[j]
            return (parts[0] + fold(jnp.where(kk > thr, 1.0, 0.0)), parts[1] + fold(jnp.where(kk == thr, 1.0, 0.0)))

        z = jnp.zeros((SUB, T), F32)
        pg, pe = lax.fori_loop(0, n_blk, body, (z, z))
        return jnp.sum(pg, axis=0, keepdims=True), jnp.sum(pe, axis=0, keepdims=True)

    cnt_gt, cnt_eq = count_gt_eq()
    need = topk - cnt_gt
    s_total = keys_scr.shape[0] * LANES
    jstar_scr[...] = jnp.full(jstar_scr.shape, s_total, I32)
    tie = jnp.max(jnp.where(cnt_gt + cnt_eq > topk, 1.0, 0.0)) > 0.0

    @pl.when(tie)
    def _():
        n_bits = max(1, (s_total - 1).bit_length())

        def jbit(bi, lo):
            cand = lo + jnp.left_shift(jnp.int32(1), n_bits - 1 - bi)

            def body(j, part):
                pos = j * LANES + lax.broadcasted_iota(I32, (LANES, T), 0)
                hit = jnp.where(keys_scr[j] == thr, 1.0, 0.0)
                return part + fold(jnp.where(pos < cand, hit, 0.0))

            part = lax.fori_loop(0, n_blk, body, jnp.zeros((SUB, T), F32))
            c = jnp.sum(part, axis=0, keepdims=True)
            return jnp.where(c < need, cand, lo)

        lo = lax.fori_loop(0, n_bits, jbit, jnp.zeros((1, T), I32))
        jstar_scr[...] = jnp.broadcast_to(lo, jstar_scr.shape)

    jstar = jstar_scr[0:1, :]

    def mask_chunk(c, carry):
        for g in range(KC // LANES):
            j = c * (KC // LANES) + g
            kk = keys_scr[j]
            k_abs = j * LANES + lax.broadcasted_iota(I32, (LANES, 1), 0)
            at_thr = jnp.where(kk == thr, jnp.where(k_abs <= jstar, 0.0, NEG), NEG)
            madd = jnp.where(kk > thr, 0.0, at_thr)
            mask_scr[j] = jnp.where(k_abs <= t_row, madd, NEG).T
        return carry

    lax.fori_loop(0, n_sc, mask_chunk, 0)

    m_scr[...] = jnp.full(m_scr.shape, NEG, F32)
    l_scr[...] = jnp.zeros(l_scr.shape, F32)
    acc_scr[...] = jnp.zeros(acc_scr.shape, F32)
    q = qlat_ref[...].reshape(H * T, A_KV_RANK)

    def attend(blk0, k0, kw, limit, bias):
        kv = ckv_ref[pl.ds(k0, kw), :]
        s = _dot_t(q, kv)
        madd = jnp.concatenate([mask_scr[blk0 + g] for g in range(kw // LANES)], axis=1)
        if limit is not None:
            s_abs = k0 + lax.broadcasted_iota(I32, (1, kw), 1)
            madd = jnp.where(s_abs < limit, madd, NEG)
        s3 = s.reshape(H, T, kw) + madd[None]
        if bias is not None:
            s3 = s3 + bias
        s = s3.reshape(H * T, kw)
        m_prev = m_scr[...]
        m_new = jnp.maximum(m_prev, jnp.max(s, axis=1, keepdims=True))
        alpha = jnp.exp2(m_prev - m_new)
        p = jnp.exp2(s - jnp.concatenate([m_new] * (kw // LANES), axis=1))
        l_scr[...] = alpha * l_scr[...] + jnp.sum(p, axis=1, keepdims=True)
        acc_scr[...] = alpha * acc_scr[...] + _dot(p.astype(BF16), kv)
        m_scr[...] = m_new

    far_end = jnp.maximum(t0 - T, 0)
    n_far = (far_end + KF - 1) // KF

    def far_chunk(c, carry):
        k0 = pl.multiple_of(c * KF, KF)
        attend(c * (KF // LANES), k0, KF, far_end, None)
        return carry

    lax.fori_loop(0, n_far, far_chunk, 0)
    near0 = pl.multiple_of(far_end, LANES)
    attend(far_end // LANES, near0, KC, None, bias_ref[0])

    o = (acc_scr[...] / l_scr[...]).astype(BF16).reshape(H, T, A_KV_RANK)
    for p in range(H // 2):
        pair = jnp.concatenate([o[2 * p], o[2 * p + 1]], axis=1)
        o_ref[:, p * LANES:(p + 1) * LANES] = _dot(pair, wuv_ref[p]).astype(BF16)


def _dsa_attn(bsz, seq, qidx, widx, kidx, ckv, qlat, bias, wuvbd):
    T = DSA_T
    nq = seq // T
    n = bsz * seq
    topk = min(IDX_TOPK_MAX, seq // IDX_TOPK_FRAC)
    kernel = functools.partial(_dsa_attn_kernel, float(topk))
    return pl.pallas_call(
        kernel,
        grid=(bsz, nq),
        in_specs=[
            pl.BlockSpec((IDX_HEADS, T, IDX_DIM), lambda b, i: (0, b * nq + i, 0)),
            pl.BlockSpec((IDX_HEADS, T), lambda b, i: (0, b * nq + i)),
            pl.BlockSpec((seq, IDX_DIM), lambda b, i: (b, 0)),
            pl.BlockSpec((seq, A_KV_RANK), lambda b, i: (b, 0)),
            pl.BlockSpec((A_HEADS, T, A_KV_RANK), lambda b, i: (0, b * nq + i, 0)),
            pl.BlockSpec((1, A_HEADS, T, DSA_KC), lambda b, i: (jnp.minimum(i, 1), 0, 0, 0)),
            pl.BlockSpec(wuvbd.shape, lambda b, i: (0, 0, 0)),
        ],
        out_specs=pl.BlockSpec((T, A_HEADS * A_V_DIM), lambda b, i: (b * nq + i, 0)),
        out_shape=jax.ShapeDtypeStruct((n, A_HEADS * A_V_DIM), BF16),
        scratch_shapes=[
            pltpu.VMEM((seq // LANES, LANES, T), I32),
            pltpu.VMEM((seq // LANES, LANES, T), jnp.int16),
            pltpu.VMEM((seq // LANES, LANES, T), jnp.int16),
            pltpu.VMEM((8, T), I32),
            pltpu.VMEM((seq // LANES, T, LANES), F32),
            pltpu.VMEM((A_HEADS * T, LANES), F32),
            pltpu.VMEM((A_HEADS * T, LANES), F32),
            pltpu.VMEM((A_HEADS * T, A_KV_RANK), F32),
        ],
        compiler_params=_cparams(("parallel", "arbitrary")),
        name="dsa_attn",
    )(qidx, widx, kidx, ckv, qlat, bias, wuvbd)


def _t5_bucket_np(dist):
    max_exact = REL_BUCKETS // 2
    n = np.maximum(dist, 0)
    ratio = np.log(np.maximum(n, 1).astype(np.float32) / np.float32(max_exact)) / np.float32(math.log(REL_MAX_DIST / max_exact))
    large = max_exact + (ratio * np.float32(REL_BUCKETS - max_exact)).astype(np.int32)
    large = np.minimum(large, REL_BUCKETS - 1)
    return np.where(n < max_exact, n, large)


def _dsa_bias_tiles(rel_table):
    T, KC, H = DSA_T, DSA_KC, A_HEADS
    far_bucket = int(_t5_bucket_np(np.array([T + 1]))[0])
    assert far_bucket == REL_BUCKETS - 1 and int(_t5_bucket_np(np.array([T - 15]))[0]) == far_bucket
    L = KC + T - 1
    delta = np.arange(L)
    d_first = (T - 1) - delta
    d_rest = (2 * T - 1) - delta
    idx = np.stack([_t5_bucket_np(d_first), _t5_bucket_np(d_rest)])
    onehot = jnp.asarray((idx[..., None] == np.arange(REL_BUCKETS)).astype(np.float32))
    vec = jnp.einsum("slb,bh->shl", onehot, (rel_table - rel_table[far_bucket]) * LOG2E,
                     precision=lax.Precision.HIGHEST)
    stream = jnp.tile(jnp.pad(vec, ((0, 0), (0, 0), (0, 1))), (1, 1, T))[..., :T * L]
    return stream.reshape(2, H, T, L)[..., T - 1:T - 1 + KC].astype(F32)


def _block_diag_pairs(w):
    h, a, b = w.shape
    z = jnp.zeros((h // 2, a, b), w.dtype)
    top = jnp.concatenate([w[0::2], z], axis=2)
    bot = jnp.concatenate([z, w[1::2]], axis=2)
    return jnp.concatenate([top, bot], axis=1)


def _dsa_mixer(x2, bsz, seq, w_in, g_q, g_kv, g_kidx, b_kidx, w_uq, w_uk, w_uv, w_qidx, rel_table):
    d = x2.shape[1]
    wcq = w_in[:, :A_Q_RANK].astype(BF16)
    wckv = w_in[:, A_Q_RANK:A_Q_RANK + A_KV_RANK].astype(BF16)
    wki = w_in[:, A_Q_RANK + A_KV_RANK:A_Q_RANK + A_KV_RANK + IDX_DIM].astype(BF16)
    wwit = jnp.transpose(w_in[:, A_Q_RANK + A_KV_RANK + IDX_DIM:]).astype(BF16)
    wuq = w_uq.reshape(A_Q_RANK, A_HEADS * A_HEAD_DIM).astype(BF16)
    wukbd = _block_diag_pairs(jnp.transpose(w_uk, (1, 2, 0))).astype(BF16)
    wuvbd = _block_diag_pairs(jnp.transpose(w_uv, (1, 0, 2))).astype(BF16)
    wqi = jnp.transpose(w_qidx, (1, 0, 2)).astype(BF16)
    row = lambda v: v.reshape(1, -1).astype(F32)
    ckv, kidx, widx, qlat, qidx = _dsa_proj(x2, wcq, wckv, wki, wwit, row(g_q), row(g_kv), row(g_kidx), row(b_kidx),
                                            wuq, wukbd, wqi)
    bias = _dsa_bias_tiles(rel_table)
    return _dsa_attn(bsz, seq, qidx, widx, kidx, ckv, qlat, bias, wuvbd)


def _fox_proj_kernel(tiles_per_seq, x_ref, wq_ref, wk_ref, wv_ref, wft_ref, bf_ref,
                     q_ref, k_ref, v_ref, cum_ref, carry_scr):
    i = pl.program_id(0)
    tn = x_ref.shape[0]

    @pl.when(i % tiles_per_seq == 0)
    def _():
        carry_scr[...] = jnp.zeros(carry_scr.shape, F32)

    xb = x_ref[...].astype(BF16)
    q_ref[...] = (_dot(xb, wq_ref[...]) * (B_HEAD_DIM ** -0.5 * LOG2E)).astype(BF16)
    k_ref[...] = _dot(xb, wk_ref[...]).astype(BF16)
    v_ref[...] = _dot(xb, wv_ref[...]).astype(BF16)
    z = _dot_t(wft_ref[...], xb) + bf_ref[...]
    logf = jnp.minimum(z, 0.0) - jnp.log(1.0 + jnp.exp(-jnp.abs(z)))
    r = lax.broadcasted_iota(I32, (tn, tn), 0)
    c = lax.broadcasted_iota(I32, (tn, tn), 1)
    tri = jnp.where(r <= c, 1.0, 0.0).astype(BF16)
    hi = logf.astype(BF16)
    lo = (logf - hi.astype(F32)).astype(BF16)
    cum = _dot(hi, tri) + _dot(lo, tri) + carry_scr[:, 0:1]
    cum_ref[...] = cum * LOG2E
    carry_scr[...] = jnp.broadcast_to(cum[:, tn - 1:tn], carry_scr.shape)


def _fox_proj(x2, seq, wq, wk, wv, wft, bf):
    n, d = x2.shape
    tn = min(TOK_TILE, seq)
    hd = B_HEADS * B_HEAD_DIM
    full = lambda a: pl.BlockSpec(a.shape, lambda i, _nd=a.ndim: (0,) * _nd)
    kernel = functools.partial(_fox_proj_kernel, seq // tn)
    return pl.pallas_call(
        kernel,
        grid=(n // tn,),
        in_specs=[pl.BlockSpec((tn, d), lambda i: (i, 0))] + [full(a) for a in (wq, wk, wv, wft, bf)],
        out_specs=[pl.BlockSpec((tn, hd), lambda i: (i, 0))] * 3 + [pl.BlockSpec((B_HEADS, tn), lambda i: (0, i))],
        out_shape=[jax.ShapeDtypeStruct((n, hd), BF16)] * 3 + [jax.ShapeDtypeStruct((B_HEADS, n), F32)],
        scratch_shapes=[pltpu.VMEM((B_HEADS, LANES), F32)],
        compiler_params=_cparams(("arbitrary",)),
        name="fox_proj",
    )(x2, wq, wk, wv, wft, bf)


def _fox_attn_kernel(q_ref, k_ref, v_ref, cum_ref, o_ref, m_scr, l_scr, acc_scr):
    T = FOX_T
    i = pl.program_id(2)
    lane = lax.broadcasted_iota(I32, (1, LANES), 1)
    lo_half = lane < B_HEAD_DIM
    q = q_ref[...]
    zero = jnp.zeros_like(q)
    q2 = (jnp.where(lo_half, q, zero), jnp.where(lo_half, zero, q))
    m_scr[...] = jnp.full(m_scr.shape, NEG, F32)
    l_scr[...] = jnp.zeros(l_scr.shape, F32)
    acc_scr[...] = jnp.zeros(acc_scr.shape, F32)

    def attend(k0, causal):
        kc = k_ref[pl.ds(k0, T), :]
        vc = v_ref[pl.ds(k0, T), :]
        cs = cum_ref[0, k0 // T]
        pv = []
        alphas = []
        for h in range(2):
            s = _dot_t(q2[h], kc) - cs[h:h + 1, :]
            if causal:
                r = lax.broadcasted_iota(I32, (T, T), 0)
                c = lax.broadcasted_iota(I32, (T, T), 1)
                s = jnp.where(c <= r, s, NEG)
            m_prev = m_scr[h]
            m_new = jnp.maximum(m_prev, jnp.max(s, axis=1, keepdims=True))
            alpha = jnp.exp2(m_prev - m_new)
            p = jnp.exp2(s - jnp.concatenate([m_new] * (T // LANES), axis=1))
            l_scr[h] = alpha * l_scr[h] + jnp.sum(p, axis=1, keepdims=True)
            m_scr[h] = m_new
            pv.append(_dot(p.astype(BF16), vc))
            alphas.append(alpha)
        acc_scr[...] = jnp.where(lo_half, alphas[0], alphas[1]) * acc_scr[...] + jnp.where(lo_half, pv[0], pv[1])

    def far(c, carry):
        attend(pl.multiple_of(c * T, T), False)
        return carry

    lax.fori_loop(0, i, far, 0)
    attend(pl.multiple_of(i * T, T), True)
    o_ref[...] = (acc_scr[...] / jnp.where(lo_half, l_scr[0], l_scr[1])).astype(BF16)


def _fox_attn(bsz, seq, q, k, v, cum3):
    T = FOX_T
    nq = seq // T
    n = bsz * seq
    pairs = B_HEADS // 2
    return pl.pallas_call(
        _fox_attn_kernel,
        grid=(bsz, pairs, nq),
        in_specs=[
            pl.BlockSpec((T, LANES), lambda b, p, i: (b * nq + i, p)),
            pl.BlockSpec((seq, LANES), lambda b, p, i: (b, p)),
            pl.BlockSpec((seq, LANES), lambda b, p, i: (b, p)),
            pl.BlockSpec((1, nq, 2, T), lambda b, p, i: (p, b, 0, 0)),
        ],
        out_specs=pl.BlockSpec((T, LANES), lambda b, p, i: (b * nq + i, p)),
        out_shape=jax.ShapeDtypeStruct((n, B_HEADS * B_HEAD_DIM), BF16),
        scratch_shapes=[
            pltpu.VMEM((2, T, LANES), F32),
            pltpu.VMEM((2, T, LANES), F32),
            pltpu.VMEM((T, LANES), F32),
        ],
        compiler_params=_cparams(("parallel", "parallel", "arbitrary")),
        name="fox_attn",
    )(q, k, v, cum3)


def _fox_mixer(x2, bsz, seq, w_in, b_f):
    hd = B_HEADS * B_HEAD_DIM
    wq = w_in[:, :hd].astype(BF16)
    wk = w_in[:, hd:2 * hd].astype(BF16)
    wv = w_in[:, 2 * hd:3 * hd].astype(BF16)
    wft = jnp.transpose(w_in[:, 3 * hd:]).astype(BF16)
    bf = b_f.reshape(B_HEADS, 1).astype(F32)
    q, k, v, cum = _fox_proj(x2, seq, wq, wk, wv, wft, bf)
    cum3 = jnp.transpose(cum.reshape(B_HEADS // 2, 2, bsz * seq // FOX_T, FOX_T), (0, 2, 1, 3))
    return _fox_attn(bsz, seq, q, k, v, cum3)


def _oproj_route_kernel(a_ref, x_ref, wo_ref, g_ref, b_ref, wr_ref, br_ref,
                        x1_ref, tope_ref, rank_ref, gate_ref, cnt_ref, carry_scr):
    i = pl.program_id(0)
    tn = x_ref.shape[0]
    ne = wr_ref.shape[1]

    @pl.when(i == 0)
    def _():
        carry_scr[...] = jnp.zeros(carry_scr.shape, F32)

    h = _dot(a_ref[...], wo_ref[...])
    x1 = _layer_norm(DEEPNORM_ALPHA * x_ref[...] + h, g_ref[...], b_ref[...])
    x1_ref[...] = x1
    logits = _dot(x1.astype(BF16), wr_ref[...]) + br_ref[...]
    lane = lax.broadcasted_iota(I32, (tn, ne), 1)
    work = logits
    vals, hots = [], []
    for k in range(TOP_K):
        m = jnp.max(work, axis=1, keepdims=True)
        idx = jnp.min(jnp.where(work == m, lane, ne), axis=1, keepdims=True)
        hot = lane == idx
        vals.append(m)
        hots.append(hot)
        tope_ref[:, k:k + 1] = idx
        work = jnp.where(hot, -jnp.inf, work)
    es = [jnp.exp(v - vals[0]) for v in vals]
    den = es[0] + es[1] + es[2] + es[3]
    for k in range(TOP_K):
        gate_ref[:, k:k + 1] = es[k] / den
    sel = jnp.zeros((tn, ne), F32)
    for hot in hots:
        sel = sel + jnp.where(hot, 1.0, 0.0)
    r = lax.broadcasted_iota(I32, (tn, tn), 0)
    c = lax.broadcasted_iota(I32, (tn, tn), 1)
    tril = jnp.where(c < r, 1.0, 0.0).astype(BF16)
    prefix = _dot(tril, sel.astype(BF16)) + carry_scr[0:1, :]
    for k in range(TOP_K):
        rank_ref[:, k:k + 1] = jnp.sum(jnp.where(hots[k], prefix, 0.0), axis=1, keepdims=True).astype(I32)
    total = carry_scr[0:1, :] + jnp.sum(sel, axis=0, keepdims=True)
    carry_scr[...] = jnp.broadcast_to(total, carry_scr.shape)
    cnt_ref[...] = total.astype(I32)


def _oproj_route(attn, x2, wo, g, b, wr, br):
    n, d = x2.shape
    tn = min(TOK_TILE, n)
    ne = wr.shape[1]
    full = lambda a: pl.BlockSpec(a.shape, lambda i, _nd=a.ndim: (0,) * _nd)
    tok = lambda w: pl.BlockSpec((tn, w), lambda i: (i, 0))
    return pl.pallas_call(
        _oproj_route_kernel,
        grid=(n // tn,),
        in_specs=[tok(attn.shape[1]), tok(d)] + [full(a) for a in (wo, g, b, wr, br)],
        out_specs=[tok(d), tok(TOP_K), tok(TOP_K), tok(TOP_K), pl.BlockSpec((1, ne), lambda i: (0, 0))],
        out_shape=[
            jax.ShapeDtypeStruct((n, d), F32),
            jax.ShapeDtypeStruct((n, TOP_K), I32),
            jax.ShapeDtypeStruct((n, TOP_K), I32),
            jax.ShapeDtypeStruct((n, TOP_K), F32),
            jax.ShapeDtypeStruct((1, ne), I32),
        ],
        scratch_shapes=[pltpu.VMEM((8, ne), F32)],
        compiler_params=_cparams(("arbitrary",)),
        name="oproj_route",
    )(attn, x2, wo, g, b, wr, br)


def _dispatch_kernel(zlo_ref, zhi_ref, tv_ref, dest_ref, x_ref, xs_hbm, zero_scr, sem, zsem):
    tn = x_ref.shape[0]
    tm = zero_scr.shape[0]

    def issue(t, carry):
        for k in range(TOP_K):
            d = dest_ref[t * TOP_K + k]
            pltpu.make_async_copy(x_ref.at[pl.ds(t, 1)], xs_hbm.at[pl.ds(d, 1)], sem).start()
        return carry

    lax.fori_loop(0, tn, issue, 0)

    @pl.when(pl.program_id(0) == 0)
    def _():
        zero_scr[...] = jnp.zeros(zero_scr.shape, zero_scr.dtype)
        ne = zlo_ref.shape[0]
        n_tiles = tv_ref.shape[0]

        def pad_row_copy(row):
            return pltpu.make_async_copy(zero_scr.at[pl.ds(0, 1)], xs_hbm.at[pl.ds(row, 1)], zsem)

        def tile_copy(t):
            return pltpu.make_async_copy(zero_scr, xs_hbm.at[pl.ds(pl.multiple_of(t * tm, tm), tm)], zsem)

        def start_pad(e, carry):
            def body(r, c):
                pad_row_copy(zlo_ref[e] + r).start()
                return c

            return lax.fori_loop(0, zhi_ref[e] - zlo_ref[e], body, carry)

        def start_tile(t, carry):
            @pl.when(tv_ref[t] == 0)
            def _():
                tile_copy(t).start()

            return carry

        def wait_pad(e, carry):
            def body(r, c):
                pad_row_copy(0).wait()
                return c

            return lax.fori_loop(0, zhi_ref[e] - zlo_ref[e], body, carry)

        def wait_tile(t, carry):
            @pl.when(tv_ref[t] == 0)
            def _():
                tile_copy(t).wait()

            return carry

        lax.fori_loop(0, ne, start_pad, 0)
        lax.fori_loop(0, n_tiles, start_tile, 0)
        lax.fori_loop(0, ne, wait_pad, 0)
        lax.fori_loop(0, n_tiles, wait_tile, 0)

    for k in range(TOP_K):
        pltpu.make_async_copy(x_ref, xs_hbm.at[pl.ds(0, tn)], sem).wait()


def _dispatch(zlo, zhi, tile_v, dest_flat, x1, n_slots):
    n, d = x1.shape
    tn = min(ROW_TILE, n)
    grid_spec = pltpu.PrefetchScalarGridSpec(
        num_scalar_prefetch=3,
        grid=(n // tn,),
        in_specs=[
            pl.BlockSpec((tn * TOP_K,), lambda i, *_: (i,), memory_space=pltpu.SMEM),
            pl.BlockSpec((tn, d), lambda i, *_: (i, 0)),
        ],
        out_specs=pl.BlockSpec(memory_space=pl.ANY),
        scratch_shapes=[pltpu.VMEM((MOE_TM, d), x1.dtype), pltpu.SemaphoreType.DMA(()), pltpu.SemaphoreType.DMA(())],
    )
    return pl.pallas_call(
        _dispatch_kernel,
        grid_spec=grid_spec,
        out_shape=jax.ShapeDtypeStruct((n_slots, d), x1.dtype),
        compiler_params=pltpu.CompilerParams(dimension_semantics=("arbitrary",), has_side_effects=True),
        name="moe_dispatch",
    )(zlo, zhi, tile_v, dest_flat, x1)


def _expert_kernel(te_ref, tv_ref, xs_ref, wgu_ref, bgu_ref, wd_ref, bd_ref, y_ref, wgu_b, wd_b):
    t = pl.program_id(0)
    f = wd_ref.shape[1]

    @pl.when((t == 0) | (te_ref[t] != te_ref[jnp.maximum(t - 1, 0)]))
    def _():
        wgu_b[...] = wgu_ref[0].astype(BF16)
        wd_b[...] = wd_ref[0].astype(BF16)

    @pl.when(tv_ref[t] > 0)
    def _():
        xb = xs_ref[...].astype(BF16)
        h = _dot(xb, wgu_b[...]) + bgu_ref[0]
        gate = jnp.minimum(h[:, :f], SWIGLU_LIMIT)
        up = jnp.clip(h[:, f:], -SWIGLU_LIMIT, SWIGLU_LIMIT)
        act = (up + 1.0) * (gate * (1.0 / (1.0 + jnp.exp(-SWIGLU_ALPHA * gate))))
        y_ref[...] = _dot(act.astype(BF16), wd_b[...]) + bd_ref[0]

    @pl.when(tv_ref[t] == 0)
    def _():
        y_ref[...] = jnp.zeros(y_ref.shape, F32)


def _experts(layer, tile_e, tile_v, xs, wgu, bgu, wd, bd):
    n_slots, d = xs.shape
    tm = MOE_TM
    f2 = wgu.shape[3]
    f = wd.shape[2]
    grid_spec = pltpu.PrefetchScalarGridSpec(
        num_scalar_prefetch=2,
        grid=(n_slots // tm,),
        in_specs=[
            pl.BlockSpec((tm, d), lambda t, te, tv: (t, 0)),
            pl.BlockSpec((None, 1, d, f2), lambda t, te, tv: (layer, te[t], 0, 0)),
            pl.BlockSpec((1, 1, f2), lambda t, te, tv: (te[t], 0, 0)),
            pl.BlockSpec((None, 1, f, d), lambda t, te, tv: (layer, te[t], 0, 0)),
            pl.BlockSpec((1, 1, d), lambda t, te, tv: (te[t], 0, 0)),
        ],
        out_specs=pl.BlockSpec((tm, d), lambda t, te, tv: (t, 0)),
        scratch_shapes=[pltpu.VMEM((d, f2), BF16), pltpu.VMEM((f, d), BF16)],
    )
    return pl.pallas_call(
        _expert_kernel,
        grid_spec=grid_spec,
        out_shape=jax.ShapeDtypeStruct((n_slots, d), F32),
        compiler_params=_cparams(("arbitrary",)),
        name="moe_experts",
    )(tile_e, tile_v, xs, wgu, bgu, wd, bd)


def _combine_kernel(dest_ref, dest_next_ref, y_hbm, gate_ref, x_ref, g_ref, b_ref, o_ref, ybuf, sem):
    tn = x_ref.shape[0]
    i = pl.program_id(0)
    slot = i % 2

    def issue(dref, s):
        def body(t, carry):
            for k in range(TOP_K):
                d = dref[t * TOP_K + k]
                pltpu.make_async_copy(y_hbm.at[pl.ds(d, 1)], ybuf.at[s, k, pl.ds(t, 1)], sem.at[s]).start()
            return carry

        lax.fori_loop(0, tn, body, 0)

    @pl.when(i == 0)
    def _():
        issue(dest_ref, 0)

    @pl.when(i + 1 < pl.num_programs(0))
    def _():
        issue(dest_next_ref, 1 - slot)

    for k in range(TOP_K):
        pltpu.make_async_copy(y_hbm.at[pl.ds(0, tn)], ybuf.at[slot, k], sem.at[slot]).wait()
    gates = gate_ref[...]
    fsum = gates[:, 0:1] * ybuf[slot, 0]
    for k in range(1, TOP_K):
        fsum = fsum + gates[:, k:k + 1] * ybuf[slot, k]
    o_ref[...] = _layer_norm(DEEPNORM_ALPHA * x_ref[...] + fsum, g_ref[...], b_ref[...])


def _combine(dest_flat, y, gates, x1, g, b):
    n, d = x1.shape
    tn = min(ROW_TILE, n)
    full = lambda a: pl.BlockSpec(a.shape, lambda i, _nd=a.ndim: (0,) * _nd)
    last = n // tn - 1
    return pl.pallas_call(
        _combine_kernel,
        grid=(n // tn,),
        in_specs=[
            pl.BlockSpec((tn * TOP_K,), lambda i: (i,), memory_space=pltpu.SMEM),
            pl.BlockSpec((tn * TOP_K,), lambda i: (jnp.minimum(i + 1, last),), memory_space=pltpu.SMEM),
            pl.BlockSpec(memory_space=pl.ANY),
            pl.BlockSpec((tn, TOP_K), lambda i: (i, 0)),
            pl.BlockSpec((tn, d), lambda i: (i, 0)),
            full(g), full(b),
        ],
        out_specs=pl.BlockSpec((tn, d), lambda i: (i, 0)),
        out_shape=jax.ShapeDtypeStruct((n, d), F32),
        scratch_shapes=[pltpu.VMEM((2, TOP_K, tn, d), F32), pltpu.SemaphoreType.DMA((2,))],
        compiler_params=_cparams(("arbitrary",)),
        name="moe_combine",
    )(dest_flat, dest_flat, y, gates, x1, g, b)


def _moe(layer, x1, top_e, rank, gates, counts, w_gu, b_gu, w_down, b_down, ln_g, ln_b):
    n, d = x1.shape
    ne = w_gu.shape[1]
    tm = MOE_TM
    n_pairs = n * TOP_K
    n_slots = -(-n_pairs // tm) * tm + ne * tm
    n_tiles = n_slots // tm
    counts = counts.reshape(ne)
    padded = (counts + tm - 1) // tm * tm
    eidx = jnp.arange(ne, dtype=I32)
    pad_ends = jnp.sum(jnp.where(eidx[None, :] <= eidx[:, None], padded[None, :], 0), axis=1)
    pad_starts = pad_ends - padded
    start_of_pair = jnp.sum(jnp.where(top_e[..., None] == eidx, pad_starts, 0), axis=-1)
    dest = (start_of_pair + rank).reshape(n_pairs).astype(I32)
    tile_start = jnp.arange(n_tiles, dtype=I32) * tm
    tile_v = (tile_start < pad_ends[ne - 1]).astype(I32)
    tile_e = jnp.minimum(jnp.sum((pad_ends[None, :] <= tile_start[:, None]).astype(I32), axis=1), ne - 1)
    xs = _dispatch((pad_starts + counts).astype(I32), pad_ends.astype(I32), tile_v, dest, x1, n_slots)
    y = _experts(layer, tile_e, tile_v, xs, w_gu, b_gu.reshape(ne, 1, -1).astype(F32),
                 w_down, b_down.reshape(ne, 1, -1).astype(F32))
    return _combine(dest, y, gates, x1, ln_g, ln_b)


def kernel(x, rel_table, a_w_in, a_g_q, a_g_kv, a_g_kidx, a_b_kidx, a_w_uq, a_w_uk, a_w_uv, a_w_qidx, a_w_o,
           b_w_in, b_b_f, b_w_o, ln_mix_g, ln_mix_b, ln_ffn_g, ln_ffn_b, w_router, b_router, w_gu, b_gu,
           w_down, b_down):
    bsz, seq, d = x.shape
    n_mixers = 2
    depth = ln_mix_g.shape[0]
    x2 = x.reshape(bsz * seq, d)
    row = lambda v: v.reshape(1, -1).astype(F32)
    for i in range(depth):
        j = i // n_mixers
        if i % n_mixers == 0:
            attn = _dsa_mixer(x2, bsz, seq, a_w_in[j], a_g_q[j], a_g_kv[j], a_g_kidx[j], a_b_kidx[j],
                              a_w_uq[j], a_w_uk[j], a_w_uv[j], a_w_qidx[j], rel_table)
            wo = a_w_o[j]
        else:
            attn = _fox_mixer(x2, bsz, seq, b_w_in[j], b_b_f[j])
            wo = b_w_o[j]
        x1, top_e, rank, gates, counts = _oproj_route(attn, x2, wo.astype(BF16), row(ln_mix_g[i]), row(ln_mix_b[i]),
                                                      w_router[i].astype(BF16), row(b_router[i]))
        x2 = _moe(i, x1, top_e, rank, gates, counts, w_gu, b_gu[i], w_down, b_down[i],
                  row(ln_ffn_g[i]), row(ln_ffn_b[i]))
    return x2.reshape(bsz, seq, d)
```

```python
import functools
import math

import numpy as np
import jax
import jax.numpy as jnp
from jax import lax
from jax.experimental import pallas as pl
from jax.experimental.pallas import tpu as pltpu

F32 = jnp.float32
BF16 = jnp.bfloat16
I32 = jnp.int32

A_HEADS = 16
A_HEAD_DIM = 64
A_V_DIM = 64
A_Q_RANK = 256
A_KV_RANK = 128
IDX_HEADS = 8
IDX_DIM = 64
IDX_TOPK_MAX = 256
IDX_TOPK_FRAC = 4
B_HEADS = 16
B_HEAD_DIM = 64
REL_BUCKETS = 32
REL_MAX_DIST = 128
TOP_K = 4
SWIGLU_ALPHA = 1.702
SWIGLU_LIMIT = 7.0
LN_EPS = 1e-5
RMS_EPS = 1e-6
DEPTH = 2
DEEPNORM_ALPHA = (2.0 * DEPTH) ** 0.25

LANES = 128
NEG = -1e30
INT_MIN = -2147483648
MASK_KEY = INT_MIN
LOG2E = math.log2(math.e)
VMEM_LIMIT = 56 * 1024 * 1024

TOK_TILE = 512
DSA_T = 128
DSA_KC = 256
DSA_KF = 512
FOX_T = 512
MOE_TM = 512
ROW_TILE = 256


def _dot(a, b):
    return jnp.dot(a, b, preferred_element_type=F32)


def _dot_t(a, b):
    return lax.dot_general(a, b, (((1,), (1,)), ((), ())), preferred_element_type=F32)


def _cparams(sem):
    return pltpu.CompilerParams(dimension_semantics=sem, vmem_limit_bytes=VMEM_LIMIT)


def _layer_norm(v, g, b):
    mu = jnp.mean(v, axis=-1, keepdims=True)
    c = v - mu
    var = jnp.mean(c * c, axis=-1, keepdims=True)
    return c * lax.rsqrt(var + LN_EPS) * g + b


def _rms_norm(v, g):
    return v * lax.rsqrt(jnp.mean(v * v, axis=-1, keepdims=True) + RMS_EPS) * g


def _dsa_proj_kernel(x_ref, wcq_ref, wckv_ref, wki_ref, wwit_ref, gq_ref, gkv_ref, gki_ref, bki_ref,
                     wuq_ref, wuk_ref, wqi_ref,
                     ckv_ref, kidx_ref, widxt_ref, qlat_ref, qidx_ref):
    xb = x_ref[...].astype(BF16)
    cq = _rms_norm(_dot(xb, wcq_ref[...]), gq_ref[...])
    ckv = _rms_norm(_dot(xb, wckv_ref[...]), gkv_ref[...])
    ckv_ref[...] = ckv.astype(BF16)
    kidx = _layer_norm(_dot(xb, wki_ref[...]), gki_ref[...], bki_ref[...])
    kidx_ref[...] = kidx.astype(BF16)
    widxt_ref[...] = _dot_t(wwit_ref[...], xb) * (IDX_HEADS ** -0.5 * IDX_DIM ** -0.5)
    cqb = cq.astype(BF16)
    qb = _dot(cqb, wuq_ref[...]).astype(BF16)
    scale = A_HEAD_DIM ** -0.5 * LOG2E
    for p in range(A_HEADS // 2):
        ql = _dot(qb[:, p * LANES:(p + 1) * LANES], wuk_ref[p]) * scale
        qlat_ref[2 * p] = ql[:, :A_KV_RANK].astype(BF16)
        qlat_ref[2 * p + 1] = ql[:, A_KV_RANK:].astype(BF16)
    for h in range(IDX_HEADS):
        qidx_ref[h] = _dot(cqb, wqi_ref[h]).astype(BF16)


def _dsa_proj(x2, wcq, wckv, wki, wwit, gq, gkv, gki, bki, wuq, wukbd, wqi):
    n, d = x2.shape
    tn = min(TOK_TILE, n)
    full = lambda a: pl.BlockSpec(a.shape, lambda i, _nd=a.ndim: (0,) * _nd)
    weights = (wcq, wckv, wki, wwit, gq, gkv, gki, bki, wuq, wukbd, wqi)
    return pl.pallas_call(
        _dsa_proj_kernel,
        grid=(n // tn,),
        in_specs=[pl.BlockSpec((tn, d), lambda i: (i, 0))] + [full(a) for a in weights],
        out_specs=[
            pl.BlockSpec((tn, A_KV_RANK), lambda i: (i, 0)),
            pl.BlockSpec((tn, IDX_DIM), lambda i: (i, 0)),
            pl.BlockSpec((IDX_HEADS, tn), lambda i: (0, i)),
            pl.BlockSpec((A_HEADS, tn, A_KV_RANK), lambda i: (0, i, 0)),
            pl.BlockSpec((IDX_HEADS, tn, IDX_DIM), lambda i: (0, i, 0)),
        ],
        out_shape=[
            jax.ShapeDtypeStruct((n, A_KV_RANK), BF16),
            jax.ShapeDtypeStruct((n, IDX_DIM), BF16),
            jax.ShapeDtypeStruct((IDX_HEADS, n), F32),
            jax.ShapeDtypeStruct((A_HEADS, n, A_KV_RANK), BF16),
            jax.ShapeDtypeStruct((IDX_HEADS, n, IDX_DIM), BF16),
        ],
        compiler_params=_cparams(("parallel",)),
        name="dsa_proj",
    )(x2, *weights)


def _dsa_select_kernel(topk, qidx_ref, widxt_ref, kidx_ref, mask_ref, keys_scr, hi_scr, lo_scr, jstar_scr):
    T, KC = DSA_T, DSA_KC
    SUB = 8
    i = pl.program_id(1)
    t0 = i * T
    n_blk = i + 1
    n_sc = (n_blk * T + KC - 1) // KC
    t_row = t0 + lax.broadcasted_iota(I32, (1, T), 1)
    wt = widxt_ref[...]
    qi = qidx_ref[...].reshape(IDX_HEADS * T, IDX_DIM)

    def score_chunk(c, carry):
        k0 = pl.multiple_of(c * KC, KC)
        s_all = _dot_t(kidx_ref[pl.ds(k0, KC), :], qi)
        acc = jnp.zeros((KC, T), F32)
        for h in range(IDX_HEADS):
            acc = acc + wt[h:h + 1, :] * jnp.maximum(s_all[:, h * T:(h + 1) * T], 0.0)
        acc = jnp.where(acc == 0.0, 0.0, acc)
        bits = pltpu.bitcast(acc, I32)
        key = bits ^ ((bits >> 31) & 0x7FFFFFFF)
        k_abs = k0 + lax.broadcasted_iota(I32, (KC, 1), 0)
        key = jnp.where(k_abs <= t_row, key, MASK_KEY)
        for g in range(KC // LANES):
            blk = key[g * LANES:(g + 1) * LANES, :]
            keys_scr[c * (KC // LANES) + g] = blk
            hi_scr[c * (KC // LANES) + g] = (blk >> 16).astype(jnp.int16)
        return carry

    lax.fori_loop(0, n_sc, score_chunk, 0)

    def fold(hit):
        return jnp.sum(hit.reshape(LANES // SUB, SUB, T), axis=0)

    PACK = 2 * SUB
    one16, zero16 = jnp.int16(1), jnp.int16(0)

    def fold16(hit):
        parts = [hit[r * PACK:(r + 1) * PACK, :] for r in range(LANES // PACK)]
        while len(parts) > 1:
            parts = [a + b for a, b in zip(parts[0::2], parts[1::2])]
        return parts[0]

    def search16(half_scr, base):
        def count_ge(cand_s):
            def body(j, part):
                part = part + fold16(jnp.where(half_scr[2 * j] >= cand_s, one16, zero16))
                return part + fold16(jnp.where(half_scr[2 * j + 1] >= cand_s, one16, zero16))

            part = lax.fori_loop(0, n_sc, body, jnp.zeros((PACK, T), jnp.int16))
            return jnp.sum(part.astype(F32), axis=0, keepdims=True)

        def bit_body(bi, ans):
            cand = ans | jnp.left_shift(jnp.int32(1), 15 - bi)
            cnt = base + count_ge((cand ^ 0x8000).astype(jnp.int16))
            return jnp.where(cnt >= topk, cand, ans)

        return lax.fori_loop(0, 16, bit_body, jnp.zeros((1, T), I32))

    ans_hi = search16(hi_scr, 0.0)
    thr_hi = (ans_hi ^ 0x8000).astype(jnp.int16)

    def low_halves(j, part):
        hi = hi_scr[j]
        lo = ((keys_scr[j] & 0xFFFF) ^ 0x8000).astype(jnp.int16)
        lo_scr[j] = jnp.where(hi == thr_hi, lo, jnp.int16(-32768))
        return part + fold16(jnp.where(hi > thr_hi, one16, zero16))

    part_hi = lax.fori_loop(0, n_sc * (KC // LANES), low_halves, jnp.zeros((PACK, T), jnp.int16))
    cnt_hi = jnp.sum(part_hi.astype(F32), axis=0, keepdims=True)
    ans_lo = search16(lo_scr, cnt_hi)
    thr = jnp.left_shift(ans_hi ^ 0x8000, 16) | ans_lo

    def count_gt_eq():
        def body(j, parts):
            kk = keys_scr[j]
            return (parts[0] + fold(jnp.where(kk > thr, 1.0, 0.0)), parts[1] + fold(jnp.where(kk == thr, 1.0, 0.0)))

        z = jnp.zeros((SUB, T), F32)
        pg, pe = lax.fori_loop(0, n_blk, body, (z, z))
        return jnp.sum(pg, axis=0, keepdims=True), jnp.sum(pe, axis=0, keepdims=True)

    cnt_gt, cnt_eq = count_gt_eq()
    need = topk - cnt_gt
    s_total = keys_scr.shape[0] * LANES
    jstar_scr[...] = jnp.full(jstar_scr.shape, s_total, I32)
    tie = jnp.max(jnp.where(cnt_gt + cnt_eq > topk, 1.0, 0.0)) > 0.0

    @pl.when(tie)
    def _():
        n_bits = max(1, (s_total - 1).bit_length())

        def jbit(bi, lo):
            cand = lo + jnp.left_shift(jnp.int32(1), n_bits - 1 - bi)

            def body(j, part):
                pos = j * LANES + lax.broadcasted_iota(I32, (LANES, T), 0)
                hit = jnp.where(keys_scr[j] == thr, 1.0, 0.0)
                return part + fold(jnp.where(pos < cand, hit, 0.0))

            part = lax.fori_loop(0, n_blk, body, jnp.zeros((SUB, T), F32))
            c = jnp.sum(part, axis=0, keepdims=True)
            return jnp.where(c < need, cand, lo)

        lo = lax.fori_loop(0, n_bits, jbit, jnp.zeros((1, T), I32))
        jstar_scr[...] = jnp.broadcast_to(lo, jstar_scr.shape)

    jstar = jstar_scr[0:1, :]

    def mask_chunk(c, carry):
        for g in range(KC // LANES):
            j = c * (KC // LANES) + g
            kk = keys_scr[j]
            k_abs = j * LANES + lax.broadcasted_iota(I32, (LANES, 1), 0)
            at_thr = jnp.where(kk == thr, jnp.where(k_abs <= jstar, 0.0, NEG), NEG)
            madd = jnp.where(kk > thr, 0.0, at_thr)
            mask_ref[0, j] = jnp.where(k_abs <= t_row, madd, NEG).T.astype(BF16)
        return carry

    lax.fori_loop(0, n_sc, mask_chunk, 0)

    def mask_rest(j, carry):
        mask_ref[0, j] = jnp.full((T, LANES), NEG, BF16)
        return carry

    lax.fori_loop(n_sc * (KC // LANES), mask_ref.shape[1], mask_rest, 0)


def _dsa_select(bsz, seq, qidx, widx, kidx):
    T = DSA_T
    nq = seq // T
    topk = min(IDX_TOPK_MAX, seq // IDX_TOPK_FRAC)
    kernel = functools.partial(_dsa_select_kernel, float(topk))
    return pl.pallas_call(
        kernel,
        grid=(bsz, nq),
        in_specs=[
            pl.BlockSpec((IDX_HEADS, T, IDX_DIM), lambda b, i: (0, b * nq + i, 0)),
            pl.BlockSpec((IDX_HEADS, T), lambda b, i: (0, b * nq + i)),
            pl.BlockSpec((seq, IDX_DIM), lambda b, i: (b, 0)),
        ],
        out_specs=pl.BlockSpec((1, seq // LANES, T, LANES), lambda b, i: (b * nq + i, 0, 0, 0)),
        out_shape=jax.ShapeDtypeStruct((bsz * nq, seq // LANES, T, LANES), BF16),
        scratch_shapes=[
            pltpu.VMEM((seq // LANES, LANES, T), I32),
            pltpu.VMEM((seq // LANES, LANES, T), jnp.int16),
            pltpu.VMEM((seq // LANES, LANES, T), jnp.int16),
            pltpu.VMEM((8, T), I32),
        ],
        compiler_params=_cparams(("parallel", "arbitrary")),
        name="dsa_select",
    )(qidx, widx, kidx)


def _dsa_attn_kernel(ckv_ref, qlat_ref, mask_ref, bias_ref, wuv_ref, o_ref, m_scr, l_scr, acc_scr):
    T, KC, KF, H = DSA_T, DSA_KC, DSA_KF, A_HEADS
    t0 = pl.program_id(1) * T
    m_scr[...] = jnp.full(m_scr.shape, NEG, F32)
    l_scr[...] = jnp.zeros(l_scr.shape, F32)
    acc_scr[...] = jnp.zeros(acc_scr.shape, F32)
    q = qlat_ref[...].reshape(H * T, A_KV_RANK)

    def attend(blk0, k0, kw, limit, bias):
        kv = ckv_ref[pl.ds(k0, kw), :]
        s = _dot_t(q, kv)
        madd = jnp.concatenate([mask_ref[0, blk0 + g] for g in range(kw // LANES)], axis=1).astype(F32)
        if limit is not None:
            s_abs = k0 + lax.broadcasted_iota(I32, (1, kw), 1)
            madd = jnp.where(s_abs < limit, madd, NEG)
        s3 = s.reshape(H, T, kw) + madd[None]
        if bias is not None:
            s3 = s3 + bias
        s = s3.reshape(H * T, kw)
        m_prev = m_scr[...]
        m_new = jnp.maximum(m_prev, jnp.max(s, axis=1, keepdims=True))
        alpha = jnp.exp2(m_prev - m_new)
        p = jnp.exp2(s - jnp.concatenate([m_new] * (kw // LANES), axis=1))
        l_scr[...] = alpha * l_scr[...] + jnp.sum(p, axis=1, keepdims=True)
        acc_scr[...] = alpha * acc_scr[...] + _dot(p.astype(BF16), kv)
        m_scr[...] = m_new

    far_end = jnp.maximum(t0 - T, 0)
    n_far = (far_end + KF - 1) // KF

    def far_chunk(c, carry):
        k0 = pl.multiple_of(c * KF, KF)
        attend(c * (KF // LANES), k0, KF, far_end, None)
        return carry

    lax.fori_loop(0, n_far, far_chunk, 0)
    near0 = pl.multiple_of(far_end, LANES)
    attend(far_end // LANES, near0, KC, None, bias_ref[0])

    o = (acc_scr[...] / l_scr[...]).astype(BF16).reshape(H, T, A_KV_RANK)
    for p in range(H // 2):
        pair = jnp.concatenate([o[2 * p], o[2 * p + 1]], axis=1)
        o_ref[:, p * LANES:(p + 1) * LANES] = _dot(pair, wuv_ref[p]).astype(BF16)


def _dsa_attn(bsz, seq, ckv, qlat, mask, bias, wuvbd):
    T = DSA_T
    nq = seq // T
    n = bsz * seq
    return pl.pallas_call(
        _dsa_attn_kernel,
        grid=(bsz, nq),
        in_specs=[
            pl.BlockSpec((seq, A_KV_RANK), lambda b, i: (b, 0)),
            pl.BlockSpec((A_HEADS, T, A_KV_RANK), lambda b, i: (0, b * nq + i, 0)),
            pl.BlockSpec((1, seq // LANES, T, LANES), lambda b, i: (b * nq + i, 0, 0, 0)),
            pl.BlockSpec((1, A_HEADS, T, DSA_KC), lambda b, i: (jnp.minimum(i, 1), 0, 0, 0)),
            pl.BlockSpec(wuvbd.shape, lambda b, i: (0, 0, 0)),
        ],
        out_specs=pl.BlockSpec((T, A_HEADS * A_V_DIM), lambda b, i: (b * nq + i, 0)),
        out_shape=jax.ShapeDtypeStruct((n, A_HEADS * A_V_DIM), BF16),
        scratch_shapes=[
            pltpu.VMEM((A_HEADS * T, LANES), F32),
            pltpu.VMEM((A_HEADS * T, LANES), F32),
            pltpu.VMEM((A_HEADS * T, A_KV_RANK), F32),
        ],
        compiler_params=_cparams(("parallel", "arbitrary")),
        name="dsa_attn",
    )(ckv, qlat, mask, bias, wuvbd)


def _t5_bucket_np(dist):
    max_exact = REL_BUCKETS // 2
    n = np.maximum(dist, 0)
    ratio = np.log(np.maximum(n, 1).astype(np.float32) / np.float32(max_exact)) / np.float32(math.log(REL_MAX_DIST / max_exact))
    large = max_exact + (ratio * np.float32(REL_BUCKETS - max_exact)).astype(np.int32)
    large = np.minimum(large, REL_BUCKETS - 1)
    return np.where(n < max_exact, n, large)


def _dsa_bias_tiles(rel_table):
    T, KC, H = DSA_T, DSA_KC, A_HEADS
    far_bucket = int(_t5_bucket_np(np.array([T + 1]))[0])
    assert far_bucket == REL_BUCKETS - 1 and int(_t5_bucket_np(np.array([T - 15]))[0]) == far_bucket
    L = KC + T - 1
    delta = np.arange(L)
    d_first = (T - 1) - delta
    d_rest = (2 * T - 1) - delta
    idx = np.stack([_t5_bucket_np(d_first), _t5_bucket_np(d_rest)])
    onehot = jnp.asarray((idx[..., None] == np.arange(REL_BUCKETS)).astype(np.float32))
    vec = jnp.einsum("slb,bh->shl", onehot, (rel_table - rel_table[far_bucket]) * LOG2E,
                     precision=lax.Precision.HIGHEST)
    stream = jnp.tile(jnp.pad(vec, ((0, 0), (0, 0), (0, 1))), (1, 1, T))[..., :T * L]
    return stream.reshape(2, H, T, L)[..., T - 1:T - 1 + KC].astype(F32)


def _block_diag_pairs(w):
    h, a, b = w.shape
    z = jnp.zeros((h // 2, a, b), w.dtype)
    top = jnp.concatenate([w[0::2], z], axis=2)
    bot = jnp.concatenate([z, w[1::2]], axis=2)
    return jnp.concatenate([top, bot], axis=1)


def _dsa_mixer(x2, bsz, seq, w_in, g_q, g_kv, g_kidx, b_kidx, w_uq, w_uk, w_uv, w_qidx, rel_table):
    d = x2.shape[1]
    wcq = w_in[:, :A_Q_RANK].astype(BF16)
    wckv = w_in[:, A_Q_RANK:A_Q_RANK + A_KV_RANK].astype(BF16)
    wki = w_in[:, A_Q_RANK + A_KV_RANK:A_Q_RANK + A_KV_RANK + IDX_DIM].astype(BF16)
    wwit = jnp.transpose(w_in[:, A_Q_RANK + A_KV_RANK + IDX_DIM:]).astype(BF16)
    wuq = w_uq.reshape(A_Q_RANK, A_HEADS * A_HEAD_DIM).astype(BF16)
    wukbd = _block_diag_pairs(jnp.transpose(w_uk, (1, 2, 0))).astype(BF16)
    wuvbd = _block_diag_pairs(jnp.transpose(w_uv, (1, 0, 2))).astype(BF16)
    wqi = jnp.transpose(w_qidx, (1, 0, 2)).astype(BF16)
    row = lambda v: v.reshape(1, -1).astype(F32)
    ckv, kidx, widx, qlat, qidx = _dsa_proj(x2, wcq, wckv, wki, wwit, row(g_q), row(g_kv), row(g_kidx), row(b_kidx),
                                            wuq, wukbd, wqi)
    bias = _dsa_bias_tiles(rel_table)
    mask = _dsa_select(bsz, seq, qidx, widx, kidx)
    return _dsa_attn(bsz, seq, ckv, qlat, mask, bias, wuvbd)


def _fox_proj_kernel(tiles_per_seq, x_ref, wq_ref, wk_ref, wv_ref, wft_ref, bf_ref,
                     q_ref, k_ref, v_ref, cum_ref, carry_scr):
    i = pl.program_id(0)
    tn = x_ref.shape[0]

    @pl.when(i % tiles_per_seq == 0)
    def _():
        carry_scr[...] = jnp.zeros(carry_scr.shape, F32)

    xb = x_ref[...].astype(BF16)
    q_ref[...] = (_dot(xb, wq_ref[...]) * (B_HEAD_DIM ** -0.5 * LOG2E)).astype(BF16)
    k_ref[...] = _dot(xb, wk_ref[...]).astype(BF16)
    v_ref[...] = _dot(xb, wv_ref[...]).astype(BF16)
    z = _dot_t(wft_ref[...], xb) + bf_ref[...]
    logf = jnp.minimum(z, 0.0) - jnp.log(1.0 + jnp.exp(-jnp.abs(z)))
    r = lax.broadcasted_iota(I32, (tn, tn), 0)
    c = lax.broadcasted_iota(I32, (tn, tn), 1)
    tri = jnp.where(r <= c, 1.0, 0.0).astype(BF16)
    hi = logf.astype(BF16)
    lo = (logf - hi.astype(F32)).astype(BF16)
    cum = _dot(hi, tri) + _dot(lo, tri) + carry_scr[:, 0:1]
    cum_ref[...] = cum * LOG2E
    carry_scr[...] = jnp.broadcast_to(cum[:, tn - 1:tn], carry_scr.shape)


def _fox_proj(x2, seq, wq, wk, wv, wft, bf):
    n, d = x2.shape
    tn = min(TOK_TILE, seq)
    hd = B_HEADS * B_HEAD_DIM
    full = lambda a: pl.BlockSpec(a.shape, lambda i, _nd=a.ndim: (0,) * _nd)
    kernel = functools.partial(_fox_proj_kernel, seq // tn)
    return pl.pallas_call(
        kernel,
        grid=(n // tn,),
        in_specs=[pl.BlockSpec((tn, d), lambda i: (i, 0))] + [full(a) for a in (wq, wk, wv, wft, bf)],
        out_specs=[pl.BlockSpec((tn, hd), lambda i: (i, 0))] * 3 + [pl.BlockSpec((B_HEADS, tn), lambda i: (0, i))],
        out_shape=[jax.ShapeDtypeStruct((n, hd), BF16)] * 3 + [jax.ShapeDtypeStruct((B_HEADS, n), F32)],
        scratch_shapes=[pltpu.VMEM((B_HEADS, LANES), F32)],
        compiler_params=_cparams(("arbitrary",)),
        name="fox_proj",
    )(x2, wq, wk, wv, wft, bf)


def _fox_attn_kernel(q_ref, k_ref, v_ref, cum_ref, o_ref, m_scr, l_scr, acc_scr):
    T = FOX_T
    i = pl.program_id(2)
    lane = lax.broadcasted_iota(I32, (1, LANES), 1)
    lo_half = lane < B_HEAD_DIM
    q = q_ref[...]
    zero = jnp.zeros_like(q)
    q2 = (jnp.where(lo_half, q, zero), jnp.where(lo_half, zero, q))
    m_scr[...] = jnp.full(m_scr.shape, NEG, F32)
    l_scr[...] = jnp.zeros(l_scr.shape, F32)
    acc_scr[...] = jnp.zeros(acc_scr.shape, F32)

    def attend(k0, causal):
        kc = k_ref[pl.ds(k0, T), :]
        vc = v_ref[pl.ds(k0, T), :]
        cs = cum_ref[0, k0 // T]
        pv = []
        alphas = []
        for h in range(2):
            s = _dot_t(q2[h], kc) - cs[h:h + 1, :]
            if causal:
                r = lax.broadcasted_iota(I32, (T, T), 0)
                c = lax.broadcasted_iota(I32, (T, T), 1)
                s = jnp.where(c <= r, s, NEG)
            m_prev = m_scr[h]
            m_new = jnp.maximum(m_prev, jnp.max(s, axis=1, keepdims=True))
            alpha = jnp.exp2(m_prev - m_new)
            p = jnp.exp2(s - jnp.concatenate([m_new] * (T // LANES), axis=1))
            l_scr[h] = alpha * l_scr[h] + jnp.sum(p, axis=1, keepdims=True)
            m_scr[h] = m_new
            pv.append(_dot(p.astype(BF16), vc))
            alphas.append(alpha)
        acc_scr[...] = jnp.where(lo_half, alphas[0], alphas[1]) * acc_scr[...] + jnp.where(lo_half, pv[0], pv[1])

    def far(c, carry):
        attend(pl.multiple_of(c * T, T), False)
        return carry

    lax.fori_loop(0, i, far, 0)
    attend(pl.multiple_of(i * T, T), True)
    o_ref[...] = (acc_scr[...] / jnp.where(lo_half, l_scr[0], l_scr[1])).astype(BF16)


def _fox_attn(bsz, seq, q, k, v, cum3):
    T = FOX_T
    nq = seq // T
    n = bsz * seq
    pairs = B_HEADS // 2
    return pl.pallas_call(
        _fox_attn_kernel,
        grid=(bsz, pairs, nq),
        in_specs=[
            pl.BlockSpec((T, LANES), lambda b, p, i: (b * nq + i, p)),
            pl.BlockSpec((seq, LANES), lambda b, p, i: (b, p)),
            pl.BlockSpec((seq, LANES), lambda b, p, i: (b, p)),
            pl.BlockSpec((1, nq, 2, T), lambda b, p, i: (p, b, 0, 0)),
        ],
        out_specs=pl.BlockSpec((T, LANES), lambda b, p, i: (b * nq + i, p)),
        out_shape=jax.ShapeDtypeStruct((n, B_HEADS * B_HEAD_DIM), BF16),
        scratch_shapes=[
            pltpu.VMEM((2, T, LANES), F32),
            pltpu.VMEM((2, T, LANES), F32),
            pltpu.VMEM((T, LANES), F32),
        ],
        compiler_params=_cparams(("parallel", "parallel", "arbitrary")),
        name="fox_attn",
    )(q, k, v, cum3)


def _fox_mixer(x2, bsz, seq, w_in, b_f):
    hd = B_HEADS * B_HEAD_DIM
    wq = w_in[:, :hd].astype(BF16)
    wk = w_in[:, hd:2 * hd].astype(BF16)
    wv = w_in[:, 2 * hd:3 * hd].astype(BF16)
    wft = jnp.transpose(w_in[:, 3 * hd:]).astype(BF16)
    bf = b_f.reshape(B_HEADS, 1).astype(F32)
    q, k, v, cum = _fox_proj(x2, seq, wq, wk, wv, wft, bf)
    cum3 = jnp.transpose(cum.reshape(B_HEADS // 2, 2, bsz * seq // FOX_T, FOX_T), (0, 2, 1, 3))
    return _fox_attn(bsz, seq, q, k, v, cum3)


def _oproj_route_kernel(a_ref, x_ref, wo_ref, g_ref, b_ref, wr_ref, br_ref,
                        x1_ref, tope_ref, rank_ref, gate_ref, cnt_ref, carry_scr):
    i = pl.program_id(0)
    tn = x_ref.shape[0]
    ne = wr_ref.shape[1]

    @pl.when(i == 0)
    def _():
        carry_scr[...] = jnp.zeros(carry_scr.shape, F32)

    h = _dot(a_ref[...], wo_ref[...])
    x1 = _layer_norm(DEEPNORM_ALPHA * x_ref[...] + h, g_ref[...], b_ref[...])
    x1_ref[...] = x1
    logits = _dot(x1.astype(BF16), wr_ref[...]) + br_ref[...]
    lane = lax.broadcasted_iota(I32, (tn, ne), 1)
    work = logits
    vals, hots = [], []
    for k in range(TOP_K):
        m = jnp.max(work, axis=1, keepdims=True)
        idx = jnp.min(jnp.where(work == m, lane, ne), axis=1, keepdims=True)
        hot = lane == idx
        vals.append(m)
        hots.append(hot)
        tope_ref[:, k:k + 1] = idx
        work = jnp.where(hot, -jnp.inf, work)
    es = [jnp.exp(v - vals[0]) for v in vals]
    den = es[0] + es[1] + es[2] + es[3]
    for k in range(TOP_K):
        gate_ref[:, k:k + 1] = es[k] / den
    sel = jnp.zeros((tn, ne), F32)
    for hot in hots:
        sel = sel + jnp.where(hot, 1.0, 0.0)
    r = lax.broadcasted_iota(I32, (tn, tn), 0)
    c = lax.broadcasted_iota(I32, (tn, tn), 1)
    tril = jnp.where(c < r, 1.0, 0.0).astype(BF16)
    prefix = _dot(tril, sel.astype(BF16)) + carry_scr[0:1, :]
    for k in range(TOP_K):
        rank_ref[:, k:k + 1] = jnp.sum(jnp.where(hots[k], prefix, 0.0), axis=1, keepdims=True).astype(I32)
    total = carry_scr[0:1, :] + jnp.sum(sel, axis=0, keepdims=True)
    carry_scr[...] = jnp.broadcast_to(total, carry_scr.shape)
    cnt_ref[...] = total.astype(I32)


def _oproj_route(attn, x2, wo, g, b, wr, br):
    n, d = x2.shape
    tn = min(TOK_TILE, n)
    ne = wr.shape[1]
    full = lambda a: pl.BlockSpec(a.shape, lambda i, _nd=a.ndim: (0,) * _nd)
    tok = lambda w: pl.BlockSpec((tn, w), lambda i: (i, 0))
    return pl.pallas_call(
        _oproj_route_kernel,
        grid=(n // tn,),
        in_specs=[tok(attn.shape[1]), tok(d)] + [full(a) for a in (wo, g, b, wr, br)],
        out_specs=[tok(d), tok(TOP_K), tok(TOP_K), tok(TOP_K), pl.BlockSpec((1, ne), lambda i: (0, 0))],
        out_shape=[
            jax.ShapeDtypeStruct((n, d), F32),
            jax.ShapeDtypeStruct((n, TOP_K), I32),
            jax.ShapeDtypeStruct((n, TOP_K), I32),
            jax.ShapeDtypeStruct((n, TOP_K), F32),
            jax.ShapeDtypeStruct((1, ne), I32),
        ],
        scratch_shapes=[pltpu.VMEM((8, ne), F32)],
        compiler_params=_cparams(("arbitrary",)),
        name="oproj_route",
    )(attn, x2, wo, g, b, wr, br)


def _dispatch_kernel(zlo_ref, zhi_ref, tv_ref, dest_ref, x_ref, xs_hbm, zero_scr, sem, zsem):
    tn = x_ref.shape[0]
    tm = zero_scr.shape[0]

    def issue(t, carry):
        for k in range(TOP_K):
            d = dest_ref[t * TOP_K + k]
            pltpu.make_async_copy(x_ref.at[pl.ds(t, 1)], xs_hbm.at[pl.ds(d, 1)], sem).start()
        return carry

    lax.fori_loop(0, tn, issue, 0)

    @pl.when(pl.program_id(0) == 0)
    def _():
        zero_scr[...] = jnp.zeros(zero_scr.shape, zero_scr.dtype)
        ne = zlo_ref.shape[0]
        n_tiles = tv_ref.shape[0]

        def pad_row_copy(row):
            return pltpu.make_async_copy(zero_scr.at[pl.ds(0, 1)], xs_hbm.at[pl.ds(row, 1)], zsem)

        def tile_copy(t):
            return pltpu.make_async_copy(zero_scr, xs_hbm.at[pl.ds(pl.multiple_of(t * tm, tm), tm)], zsem)

        def start_pad(e, carry):
            def body(r, c):
                pad_row_copy(zlo_ref[e] + r).start()
                return c

            return lax.fori_loop(0, zhi_ref[e] - zlo_ref[e], body, carry)

        def start_tile(t, carry):
            @pl.when(tv_ref[t] == 0)
            def _():
                tile_copy(t).start()

            return carry

        def wait_pad(e, carry):
            def body(r, c):
                pad_row_copy(0).wait()
                return c

            return lax.fori_loop(0, zhi_ref[e] - zlo_ref[e], body, carry)

        def wait_tile(t, carry):
            @pl.when(tv_ref[t] == 0)
            def _():
                tile_copy(t).wait()

            return carry

        lax.fori_loop(0, ne, start_pad, 0)
        lax.fori_loop(0, n_tiles, start_tile, 0)
        lax.fori_loop(0, ne, wait_pad, 0)
        lax.fori_loop(0, n_tiles, wait_tile, 0)

    for k in range(TOP_K):
        pltpu.make_async_copy(x_ref, xs_hbm.at[pl.ds(0, tn)], sem).wait()


def _dispatch(zlo, zhi, tile_v, dest_flat, x1, n_slots):
    n, d = x1.shape
    tn = min(ROW_TILE, n)
    grid_spec = pltpu.PrefetchScalarGridSpec(
        num_scalar_prefetch=3,
        grid=(n // tn,),
        in_specs=[
            pl.BlockSpec((tn * TOP_K,), lambda i, *_: (i,), memory_space=pltpu.SMEM),
            pl.BlockSpec((tn, d), lambda i, *_: (i, 0)),
        ],
        out_specs=pl.BlockSpec(memory_space=pl.ANY),
        scratch_shapes=[pltpu.VMEM((MOE_TM, d), x1.dtype), pltpu.SemaphoreType.DMA(()), pltpu.SemaphoreType.DMA(())],
    )
    return pl.pallas_call(
        _dispatch_kernel,
        grid_spec=grid_spec,
        out_shape=jax.ShapeDtypeStruct((n_slots, d), x1.dtype),
        compiler_params=pltpu.CompilerParams(dimension_semantics=("arbitrary",), has_side_effects=True),
        name="moe_dispatch",
    )(zlo, zhi, tile_v, dest_flat, x1)


def _expert_kernel(te_ref, tv_ref, xs_ref, wgu_ref, bgu_ref, wd_ref, bd_ref, y_ref, wgu_b, wd_b):
    t = pl.program_id(0)
    f = wd_ref.shape[1]

    @pl.when((t == 0) | (te_ref[t] != te_ref[jnp.maximum(t - 1, 0)]))
    def _():
        wgu_b[...] = wgu_ref[0].astype(BF16)
        wd_b[...] = wd_ref[0].astype(BF16)

    @pl.when(tv_ref[t] > 0)
    def _():
        xb = xs_ref[...].astype(BF16)
        h = _dot(xb, wgu_b[...]) + bgu_ref[0]
        gate = jnp.minimum(h[:, :f], SWIGLU_LIMIT)
        up = jnp.clip(h[:, f:], -SWIGLU_LIMIT, SWIGLU_LIMIT)
        act = (up + 1.0) * (gate * (1.0 / (1.0 + jnp.exp(-SWIGLU_ALPHA * gate))))
        y_ref[...] = _dot(act.astype(BF16), wd_b[...]) + bd_ref[0]

    @pl.when(tv_ref[t] == 0)
    def _():
        y_ref[...] = jnp.zeros(y_ref.shape, F32)


def _experts(layer, tile_e, tile_v, xs, wgu, bgu, wd, bd):
    n_slots, d = xs.shape
    tm = MOE_TM
    f2 = wgu.shape[3]
    f = wd.shape[2]
    grid_spec = pltpu.PrefetchScalarGridSpec(
        num_scalar_prefetch=2,
        grid=(n_slots // tm,),
        in_specs=[
            pl.BlockSpec((tm, d), lambda t, te, tv: (t, 0)),
            pl.BlockSpec((None, 1, d, f2), lambda t, te, tv: (layer, te[t], 0, 0)),
            pl.BlockSpec((1, 1, f2), lambda t, te, tv: (te[t], 0, 0)),
            pl.BlockSpec((None, 1, f, d), lambda t, te, tv: (layer, te[t], 0, 0)),
            pl.BlockSpec((1, 1, d), lambda t, te, tv: (te[t], 0, 0)),
        ],
        out_specs=pl.BlockSpec((tm, d), lambda t, te, tv: (t, 0)),
        scratch_shapes=[pltpu.VMEM((d, f2), BF16), pltpu.VMEM((f, d), BF16)],
    )
    return pl.pallas_call(
        _expert_kernel,
        grid_spec=grid_spec,
        out_shape=jax.ShapeDtypeStruct((n_slots, d), F32),
        compiler_params=_cparams(("arbitrary",)),
        name="moe_experts",
    )(tile_e, tile_v, xs, wgu, bgu, wd, bd)


def _combine_kernel(dest_ref, dest_next_ref, y_hbm, gate_ref, x_ref, g_ref, b_ref, o_ref, ybuf, sem):
    tn = x_ref.shape[0]
    i = pl.program_id(0)
    slot = i % 2

    def issue(dref, s):
        def body(t, carry):
            for k in range(TOP_K):
                d = dref[t * TOP_K + k]
                pltpu.make_async_copy(y_hbm.at[pl.ds(d, 1)], ybuf.at[s, k, pl.ds(t, 1)], sem.at[s]).start()
            return carry

        lax.fori_loop(0, tn, body, 0)

    @pl.when(i == 0)
    def _():
        issue(dest_ref, 0)

    @pl.when(i + 1 < pl.num_programs(0))
    def _():
        issue(dest_next_ref, 1 - slot)

    for k in range(TOP_K):
        pltpu.make_async_copy(y_hbm.at[pl.ds(0, tn)], ybuf.at[slot, k], sem.at[slot]).wait()
    gates = gate_ref[...]
    fsum = gates[:, 0:1] * ybuf[slot, 0]
    for k in range(1, TOP_K):
        fsum = fsum + gates[:, k:k + 1] * ybuf[slot, k]
    o_ref[...] = _layer_norm(DEEPNORM_ALPHA * x_ref[...] + fsum, g_ref[...], b_ref[...])


def _combine(dest_flat, y, gates, x1, g, b):
    n, d = x1.shape
    tn = min(ROW_TILE, n)
    full = lambda a: pl.BlockSpec(a.shape, lambda i, _nd=a.ndim: (0,) * _nd)
    last = n // tn - 1
    return pl.pallas_call(
        _combine_kernel,
        grid=(n // tn,),
        in_specs=[
            pl.BlockSpec((tn * TOP_K,), lambda i: (i,), memory_space=pltpu.SMEM),
            pl.BlockSpec((tn * TOP_K,), lambda i: (jnp.minimum(i + 1, last),), memory_space=pltpu.SMEM),
            pl.BlockSpec(memory_space=pl.ANY),
            pl.BlockSpec((tn, TOP_K), lambda i: (i, 0)),
            pl.BlockSpec((tn, d), lambda i: (i, 0)),
            full(g), full(b),
        ],
        out_specs=pl.BlockSpec((tn, d), lambda i: (i, 0)),
        out_shape=jax.ShapeDtypeStruct((n, d), F32),
        scratch_shapes=[pltpu.VMEM((2, TOP_K, tn, d), F32), pltpu.SemaphoreType.DMA((2,))],
        compiler_params=_cparams(("arbitrary",)),
        name="moe_combine",
    )(dest_flat, dest_flat, y, gates, x1, g, b)


def _moe(layer, x1, top_e, rank, gates, counts, w_gu, b_gu, w_down, b_down, ln_g, ln_b):
    n, d = x1.shape
    ne = w_gu.shape[1]
    tm = MOE_TM
    n_pairs = n * TOP_K
    n_slots = -(-n_pairs // tm) * tm + ne * tm
    n_tiles = n_slots // tm
    counts = counts.reshape(ne)
    padded = (counts + tm - 1) // tm * tm
    eidx = jnp.arange(ne, dtype=I32)
    pad_ends = jnp.sum(jnp.where(eidx[None, :] <= eidx[:, None], padded[None, :], 0), axis=1)
    pad_starts = pad_ends - padded
    start_of_pair = jnp.sum(jnp.where(top_e[..., None] == eidx, pad_starts, 0), axis=-1)
    dest = (start_of_pair + rank).reshape(n_pairs).astype(I32)
    tile_start = jnp.arange(n_tiles, dtype=I32) * tm
    tile_v = (tile_start < pad_ends[ne - 1]).astype(I32)
    tile_e = jnp.minimum(jnp.sum((pad_ends[None, :] <= tile_start[:, None]).astype(I32), axis=1), ne - 1)
    xs = _dispatch((pad_starts + counts).astype(I32), pad_ends.astype(I32), tile_v, dest, x1, n_slots)
    y = _experts(layer, tile_e, tile_v, xs, w_gu, b_gu.reshape(ne, 1, -1).astype(F32),
                 w_down, b_down.reshape(ne, 1, -1).astype(F32))
    return _combine(dest, y, gates, x1, ln_g, ln_b)


def kernel(x, rel_table, a_w_in, a_g_q, a_g_kv, a_g_kidx, a_b_kidx, a_w_uq, a_w_uk, a_w_uv, a_w_qidx, a_w_o,
           b_w_in, b_b_f, b_w_o, ln_mix_g, ln_mix_b, ln_ffn_g, ln_ffn_b, w_router, b_router, w_gu, b_gu,
           w_down, b_down):
    bsz, seq, d = x.shape
    n_mixers = 2
    depth = ln_mix_g.shape[0]
    x2 = x.reshape(bsz * seq, d)
    row = lambda v: v.reshape(1, -1).astype(F32)
    for i in range(depth):
        j = i // n_mixers
        if i % n_mixers == 0:
            attn = _dsa_mixer(x2, bsz, seq, a_w_in[j], a_g_q[j], a_g_kv[j], a_g_kidx[j], a_b_kidx[j],
                              a_w_uq[j], a_w_uk[j], a_w_uv[j], a_w_qidx[j], rel_table)
            wo = a_w_o[j]
        else:
            attn = _fox_mixer(x2, bsz, seq, b_w_in[j], b_b_f[j])
            wo = b_w_o[j]
        x1, top_e, rank, gates, counts = _oproj_route(attn, x2, wo.astype(BF16), row(ln_mix_g[i]), row(ln_mix_b[i]),
                                                      w_router[i].astype(BF16), row(b_router[i]))
        x2 = _moe(i, x1, top_e, rank, gates, counts, w_gu, b_gu[i], w_down, b_down[i],
                  row(ln_ffn_g[i]), row(ln_ffn_b[i]))
    return x2.reshape(bsz, seq, d)
```

```python
import functools
import math

import numpy as np
import jax
import jax.numpy as jnp
from jax import lax
from jax.experimental import pallas as pl
from jax.experimental.pallas import tpu as pltpu

F32 = jnp.float32
BF16 = jnp.bfloat16
I32 = jnp.int32

A_HEADS = 16
A_HEAD_DIM = 64
A_V_DIM = 64
A_Q_RANK = 256
A_KV_RANK = 128
IDX_HEADS = 8
IDX_DIM = 64
IDX_TOPK_MAX = 256
IDX_TOPK_FRAC = 4
B_HEADS = 16
B_HEAD_DIM = 64
REL_BUCKETS = 32
REL_MAX_DIST = 128
TOP_K = 4
SWIGLU_ALPHA = 1.702
SWIGLU_LIMIT = 7.0
LN_EPS = 1e-5
RMS_EPS = 1e-6
DEPTH = 2
DEEPNORM_ALPHA = (2.0 * DEPTH) ** 0.25

LANES = 128
NEG = -1e30
INT_MIN = -2147483648
MASK_KEY = INT_MIN
LOG2E = math.log2(math.e)
VMEM_LIMIT = 56 * 1024 * 1024

TOK_TILE = 512
DSA_T = 128
DSA_KC = 256
DSA_KF = 512
FOX_T = 512
MOE_TM = 512
ROW_TILE = 256


def _dot(a, b):
    return jnp.dot(a, b, preferred_element_type=F32)


def _dot_t(a, b):
    return lax.dot_general(a, b, (((1,), (1,)), ((), ())), preferred_element_type=F32)


def _cparams(sem):
    return pltpu.CompilerParams(dimension_semantics=sem, vmem_limit_bytes=VMEM_LIMIT)


def _layer_norm(v, g, b):
    mu = jnp.mean(v, axis=-1, keepdims=True)
    c = v - mu
    var = jnp.mean(c * c, axis=-1, keepdims=True)
    return c * lax.rsqrt(var + LN_EPS) * g + b


def _rms_norm(v, g):
    return v * lax.rsqrt(jnp.mean(v * v, axis=-1, keepdims=True) + RMS_EPS) * g


def _dsa_proj_kernel(x_ref, wcq_ref, wckv_ref, wki_ref, wwit_ref, gq_ref, gkv_ref, gki_ref, bki_ref,
                     wuq_ref, wuk_ref, wqi_ref,
                     ckv_ref, kidx_ref, widxt_ref, qlat_ref, qidx_ref):
    xb = x_ref[...].astype(BF16)
    cq = _rms_norm(_dot(xb, wcq_ref[...]), gq_ref[...])
    ckv = _rms_norm(_dot(xb, wckv_ref[...]), gkv_ref[...])
    ckv_ref[...] = ckv.astype(BF16)
    kidx = _layer_norm(_dot(xb, wki_ref[...]), gki_ref[...], bki_ref[...])
    kidx_ref[...] = kidx.astype(BF16)
    widxt_ref[...] = _dot_t(wwit_ref[...], xb) * (IDX_HEADS ** -0.5 * IDX_DIM ** -0.5)
    cqb = cq.astype(BF16)
    qb = _dot(cqb, wuq_ref[...]).astype(BF16)
    scale = A_HEAD_DIM ** -0.5 * LOG2E
    for p in range(A_HEADS // 2):
        ql = _dot(qb[:, p * LANES:(p + 1) * LANES], wuk_ref[p]) * scale
        qlat_ref[2 * p] = ql[:, :A_KV_RANK].astype(BF16)
        qlat_ref[2 * p + 1] = ql[:, A_KV_RANK:].astype(BF16)
    for h in range(IDX_HEADS):
        qidx_ref[h] = _dot(cqb, wqi_ref[h]).astype(BF16)


def _dsa_proj(x2, wcq, wckv, wki, wwit, gq, gkv, gki, bki, wuq, wukbd, wqi):
    n, d = x2.shape
    tn = min(TOK_TILE, n)
    full = lambda a: pl.BlockSpec(a.shape, lambda i, _nd=a.ndim: (0,) * _nd)
    weights = (wcq, wckv, wki, wwit, gq, gkv, gki, bki, wuq, wukbd, wqi)
    return pl.pallas_call(
        _dsa_proj_kernel,
        grid=(n // tn,),
        in_specs=[pl.BlockSpec((tn, d), lambda i: (i, 0))] + [full(a) for a in weights],
        out_specs=[
            pl.BlockSpec((tn, A_KV_RANK), lambda i: (i, 0)),
            pl.BlockSpec((tn, IDX_DIM), lambda i: (i, 0)),
            pl.BlockSpec((IDX_HEADS, tn), lambda i: (0, i)),
            pl.BlockSpec((A_HEADS, tn, A_KV_RANK), lambda i: (0, i, 0)),
            pl.BlockSpec((IDX_HEADS, tn, IDX_DIM), lambda i: (0, i, 0)),
        ],
        out_shape=[
            jax.ShapeDtypeStruct((n, A_KV_RANK), BF16),
            jax.ShapeDtypeStruct((n, IDX_DIM), BF16),
            jax.ShapeDtypeStruct((IDX_HEADS, n), F32),
            jax.ShapeDtypeStruct((A_HEADS, n, A_KV_RANK), BF16),
            jax.ShapeDtypeStruct((IDX_HEADS, n, IDX_DIM), BF16),
        ],
        compiler_params=_cparams(("parallel",)),
        name="dsa_proj",
    )(x2, *weights)


def _dsa_select_kernel(topk, qidx_ref, widxt_ref, kidx_ref, mask_ref, keys_scr, hi_scr, lo_scr, jstar_scr):
    T, KC = DSA_T, DSA_KC
    SUB = 8
    i = pl.program_id(1)
    t0 = i * T
    n_blk = i + 1
    n_sc = (n_blk * T + KC - 1) // KC
    t_row = t0 + lax.broadcasted_iota(I32, (1, T), 1)
    wt = widxt_ref[...]
    qi = qidx_ref[...].reshape(IDX_HEADS * T, IDX_DIM)

    def score_chunk(c, carry):
        k0 = pl.multiple_of(c * KC, KC)
        s_all = _dot_t(kidx_ref[pl.ds(k0, KC), :], qi)
        acc = jnp.zeros((KC, T), F32)
        for h in range(IDX_HEADS):
            acc = acc + wt[h:h + 1, :] * jnp.maximum(s_all[:, h * T:(h + 1) * T], 0.0)
        acc = jnp.where(acc == 0.0, 0.0, acc)
        bits = pltpu.bitcast(acc, I32)
        key = bits ^ ((bits >> 31) & 0x7FFFFFFF)
        k_abs = k0 + lax.broadcasted_iota(I32, (KC, 1), 0)
        key = jnp.where(k_abs <= t_row, key, MASK_KEY)
        for g in range(KC // LANES):
            blk = key[g * LANES:(g + 1) * LANES, :]
            keys_scr[c * (KC // LANES) + g] = blk
            hi_scr[c * (KC // LANES) + g] = (blk >> 16).astype(jnp.int16)
        return carry

    lax.fori_loop(0, n_sc, score_chunk, 0)

    UNR = min(8, keys_scr.shape[0])
    n_scored = n_sc * (KC // LANES)
    n_it = (n_scored + UNR - 1) // UNR

    def mask_fill(j, carry):
        keys_scr[j] = jnp.full((LANES, T), MASK_KEY, I32)
        hi_scr[j] = jnp.full((LANES, T), -32768, jnp.int16)
        return carry

    lax.fori_loop(n_scored, n_it * UNR, mask_fill, 0)

    def fold(hit):
        return jnp.sum(hit.reshape(LANES // SUB, SUB, T), axis=0)

    PACK = 2 * SUB
    one16, zero16 = jnp.int16(1), jnp.int16(0)

    def fold16(hit):
        parts = [hit[r * PACK:(r + 1) * PACK, :] for r in range(LANES // PACK)]
        while len(parts) > 1:
            parts = [a + b for a, b in zip(parts[0::2], parts[1::2])]
        return parts[0]

    def search16(half_scr, base):
        def count_ge(cand_s):
            def body(j, part):
                for g in range(UNR):
                    part = part + fold16(jnp.where(half_scr[UNR * j + g] >= cand_s, one16, zero16))
                return part

            part = lax.fori_loop(0, n_it, body, jnp.zeros((PACK, T), jnp.int16))
            return jnp.sum(part.astype(F32), axis=0, keepdims=True)

        def bit_body(bi, ans):
            cand = ans | jnp.left_shift(jnp.int32(1), 15 - bi)
            cnt = base + count_ge((cand ^ 0x8000).astype(jnp.int16))
            return jnp.where(cnt >= topk, cand, ans)

        return lax.fori_loop(0, 16, bit_body, jnp.zeros((1, T), I32))

    ans_hi = search16(hi_scr, 0.0)
    thr_hi = (ans_hi ^ 0x8000).astype(jnp.int16)

    def low_halves(c, part):
        for g in range(UNR):
            j = UNR * c + g
            hi = hi_scr[j]
            lo = ((keys_scr[j] & 0xFFFF) ^ 0x8000).astype(jnp.int16)
            lo_scr[j] = jnp.where(hi == thr_hi, lo, jnp.int16(-32768))
            part = part + fold16(jnp.where(hi > thr_hi, one16, zero16))
        return part

    part_hi = lax.fori_loop(0, n_it, low_halves, jnp.zeros((PACK, T), jnp.int16))
    cnt_hi = jnp.sum(part_hi.astype(F32), axis=0, keepdims=True)
    ans_lo = search16(lo_scr, cnt_hi)
    thr = jnp.left_shift(ans_hi ^ 0x8000, 16) | ans_lo

    def count_gt_eq():
        def body(j, parts):
            kk = keys_scr[j]
            return (parts[0] + fold(jnp.where(kk > thr, 1.0, 0.0)), parts[1] + fold(jnp.where(kk == thr, 1.0, 0.0)))

        z = jnp.zeros((SUB, T), F32)
        pg, pe = lax.fori_loop(0, n_blk, body, (z, z))
        return jnp.sum(pg, axis=0, keepdims=True), jnp.sum(pe, axis=0, keepdims=True)

    cnt_gt, cnt_eq = count_gt_eq()
    need = topk - cnt_gt
    s_total = keys_scr.shape[0] * LANES
    jstar_scr[...] = jnp.full(jstar_scr.shape, s_total, I32)
    tie = jnp.max(jnp.where(cnt_gt + cnt_eq > topk, 1.0, 0.0)) > 0.0

    @pl.when(tie)
    def _():
        n_bits = max(1, (s_total - 1).bit_length())

        def jbit(bi, lo):
            cand = lo + jnp.left_shift(jnp.int32(1), n_bits - 1 - bi)

            def body(j, part):
                pos = j * LANES + lax.broadcasted_iota(I32, (LANES, T), 0)
                hit = jnp.where(keys_scr[j] == thr, 1.0, 0.0)
                return part + fold(jnp.where(pos < cand, hit, 0.0))

            part = lax.fori_loop(0, n_blk, body, jnp.zeros((SUB, T), F32))
            c = jnp.sum(part, axis=0, keepdims=True)
            return jnp.where(c < need, cand, lo)

        lo = lax.fori_loop(0, n_bits, jbit, jnp.zeros((1, T), I32))
        jstar_scr[...] = jnp.broadcast_to(lo, jstar_scr.shape)

    jstar = jstar_scr[0:1, :]

    def mask_chunk(c, carry):
        for g in range(KC // LANES):
            j = c * (KC // LANES) + g
            kk = keys_scr[j]
            k_abs = j * LANES + lax.broadcasted_iota(I32, (LANES, 1), 0)
            at_thr = jnp.where(kk == thr, jnp.where(k_abs <= jstar, 0.0, NEG), NEG)
            madd = jnp.where(kk > thr, 0.0, at_thr)
            mask_ref[0, j] = jnp.where(k_abs <= t_row, madd, NEG).T.astype(BF16)
        return carry

    lax.fori_loop(0, n_sc, mask_chunk, 0)

    def mask_rest(j, carry):
        mask_ref[0, j] = jnp.full((T, LANES), NEG, BF16)
        return carry

    lax.fori_loop(n_sc * (KC // LANES), mask_ref.shape[1], mask_rest, 0)


def _dsa_select(bsz, seq, qidx, widx, kidx):
    T = DSA_T
    nq = seq // T
    topk = min(IDX_TOPK_MAX, seq // IDX_TOPK_FRAC)
    kernel = functools.partial(_dsa_select_kernel, float(topk))
    return pl.pallas_call(
        kernel,
        grid=(bsz, nq),
        in_specs=[
            pl.BlockSpec((IDX_HEADS, T, IDX_DIM), lambda b, i: (0, b * nq + i, 0)),
            pl.BlockSpec((IDX_HEADS, T), lambda b, i: (0, b * nq + i)),
            pl.BlockSpec((seq, IDX_DIM), lambda b, i: (b, 0)),
        ],
        out_specs=pl.BlockSpec((1, seq // LANES, T, LANES), lambda b, i: (b * nq + i, 0, 0, 0)),
        out_shape=jax.ShapeDtypeStruct((bsz * nq, seq // LANES, T, LANES), BF16),
        scratch_shapes=[
            pltpu.VMEM((seq // LANES, LANES, T), I32),
            pltpu.VMEM((seq // LANES, LANES, T), jnp.int16),
            pltpu.VMEM((seq // LANES, LANES, T), jnp.int16),
            pltpu.VMEM((8, T), I32),
        ],
        compiler_params=_cparams(("parallel", "arbitrary")),
        name="dsa_select",
    )(qidx, widx, kidx)


def _dsa_attn_kernel(ckv_ref, qlat_ref, mask_ref, bias_ref, wuv_ref, o_ref, m_scr, l_scr, acc_scr):
    T, KC, KF, H = DSA_T, DSA_KC, DSA_KF, A_HEADS
    t0 = pl.program_id(1) * T
    m_scr[...] = jnp.full(m_scr.shape, NEG, F32)
    l_scr[...] = jnp.zeros(l_scr.shape, F32)
    acc_scr[...] = jnp.zeros(acc_scr.shape, F32)
    q = qlat_ref[...].reshape(H * T, A_KV_RANK)

    def attend(blk0, k0, kw, limit, bias):
        kv = ckv_ref[pl.ds(k0, kw), :]
        s = _dot_t(q, kv)
        madd = jnp.concatenate([mask_ref[0, blk0 + g] for g in range(kw // LANES)], axis=1).astype(F32)
        if limit is not None:
            s_abs = k0 + lax.broadcasted_iota(I32, (1, kw), 1)
            madd = jnp.where(s_abs < limit, madd, NEG)
        s3 = s.reshape(H, T, kw) + madd[None]
        if bias is not None:
            s3 = s3 + bias
        s = s3.reshape(H * T, kw)
        m_prev = m_scr[...]
        m_new = jnp.maximum(m_prev, jnp.max(s, axis=1, keepdims=True))
        alpha = jnp.exp2(m_prev - m_new)
        p = jnp.exp2(s - jnp.concatenate([m_new] * (kw // LANES), axis=1))
        l_scr[...] = alpha * l_scr[...] + jnp.sum(p, axis=1, keepdims=True)
        acc_scr[...] = alpha * acc_scr[...] + _dot(p.astype(BF16), kv)
        m_scr[...] = m_new

    far_end = jnp.maximum(t0 - T, 0)
    n_far = (far_end + KF - 1) // KF

    def far_chunk(c, carry):
        k0 = pl.multiple_of(c * KF, KF)
        attend(c * (KF // LANES), k0, KF, far_end, None)
        return carry

    lax.fori_loop(0, n_far, far_chunk, 0)
    near0 = pl.multiple_of(far_end, LANES)
    attend(far_end // LANES, near0, KC, None, bias_ref[0])

    o = (acc_scr[...] / l_scr[...]).astype(BF16).reshape(H, T, A_KV_RANK)
    for p in range(H // 2):
        pair = jnp.concatenate([o[2 * p], o[2 * p + 1]], axis=1)
        o_ref[:, p * LANES:(p + 1) * LANES] = _dot(pair, wuv_ref[p]).astype(BF16)


def _dsa_attn(bsz, seq, ckv, qlat, mask, bias, wuvbd):
    T = DSA_T
    nq = seq // T
    n = bsz * seq
    return pl.pallas_call(
        _dsa_attn_kernel,
        grid=(bsz, nq),
        in_specs=[
            pl.BlockSpec((seq, A_KV_RANK), lambda b, i: (b, 0)),
            pl.BlockSpec((A_HEADS, T, A_KV_RANK), lambda b, i: (0, b * nq + i, 0)),
            pl.BlockSpec((1, seq // LANES, T, LANES), lambda b, i: (b * nq + i, 0, 0, 0)),
            pl.BlockSpec((1, A_HEADS, T, DSA_KC), lambda b, i: (jnp.minimum(i, 1), 0, 0, 0)),
            pl.BlockSpec(wuvbd.shape, lambda b, i: (0, 0, 0)),
        ],
        out_specs=pl.BlockSpec((T, A_HEADS * A_V_DIM), lambda b, i: (b * nq + i, 0)),
        out_shape=jax.ShapeDtypeStruct((n, A_HEADS * A_V_DIM), BF16),
        scratch_shapes=[
            pltpu.VMEM((A_HEADS * T, LANES), F32),
            pltpu.VMEM((A_HEADS * T, LANES), F32),
            pltpu.VMEM((A_HEADS * T, A_KV_RANK), F32),
        ],
        compiler_params=_cparams(("parallel", "arbitrary")),
        name="dsa_attn",
    )(ckv, qlat, mask, bias, wuvbd)


def _t5_bucket_np(dist):
    max_exact = REL_BUCKETS // 2
    n = np.maximum(dist, 0)
    ratio = np.log(np.maximum(n, 1).astype(np.float32) / np.float32(max_exact)) / np.float32(math.log(REL_MAX_DIST / max_exact))
    large = max_exact + (ratio * np.float32(REL_BUCKETS - max_exact)).astype(np.int32)
    large = np.minimum(large, REL_BUCKETS - 1)
    return np.where(n < max_exact, n, large)


def _dsa_bias_tiles(rel_table):
    T, KC, H = DSA_T, DSA_KC, A_HEADS
    far_bucket = int(_t5_bucket_np(np.array([T + 1]))[0])
    assert far_bucket == REL_BUCKETS - 1 and int(_t5_bucket_np(np.array([T - 15]))[0]) == far_bucket
    L = KC + T - 1
    delta = np.arange(L)
    d_first = (T - 1) - delta
    d_rest = (2 * T - 1) - delta
    idx = np.stack([_t5_bucket_np(d_first), _t5_bucket_np(d_rest)])
    onehot = jnp.asarray((idx[..., None] == np.arange(REL_BUCKETS)).astype(np.float32))
    vec = jnp.einsum("slb,bh->shl", onehot, (rel_table - rel_table[far_bucket]) * LOG2E,
                     precision=lax.Precision.HIGHEST)
    stream = jnp.tile(jnp.pad(vec, ((0, 0), (0, 0), (0, 1))), (1, 1, T))[..., :T * L]
    return stream.reshape(2, H, T, L)[..., T - 1:T - 1 + KC].astype(F32)


def _block_diag_pairs(w):
    h, a, b = w.shape
    z = jnp.zeros((h // 2, a, b), w.dtype)
    top = jnp.concatenate([w[0::2], z], axis=2)
    bot = jnp.concatenate([z, w[1::2]], axis=2)
    return jnp.concatenate([top, bot], axis=1)


def _dsa_mixer(x2, bsz, seq, w_in, g_q, g_kv, g_kidx, b_kidx, w_uq, w_uk, w_uv, w_qidx, rel_table):
    d = x2.shape[1]
    wcq = w_in[:, :A_Q_RANK].astype(BF16)
    wckv = w_in[:, A_Q_RANK:A_Q_RANK + A_KV_RANK].astype(BF16)
    wki = w_in[:, A_Q_RANK + A_KV_RANK:A_Q_RANK + A_KV_RANK + IDX_DIM].astype(BF16)
    wwit = jnp.transpose(w_in[:, A_Q_RANK + A_KV_RANK + IDX_DIM:]).astype(BF16)
    wuq = w_uq.reshape(A_Q_RANK, A_HEADS * A_HEAD_DIM).astype(BF16)
    wukbd = _block_diag_pairs(jnp.transpose(w_uk, (1, 2, 0))).astype(BF16)
    wuvbd = _block_diag_pairs(jnp.transpose(w_uv, (1, 0, 2))).astype(BF16)
    wqi = jnp.transpose(w_qidx, (1, 0, 2)).astype(BF16)
    row = lambda v: v.reshape(1, -1).astype(F32)
    ckv, kidx, widx, qlat, qidx = _dsa_proj(x2, wcq, wckv, wki, wwit, row(g_q), row(g_kv), row(g_kidx), row(b_kidx),
                                            wuq, wukbd, wqi)
    bias = _dsa_bias_tiles(rel_table)
    mask = _dsa_select(bsz, seq, qidx, widx, kidx)
    return _dsa_attn(bsz, seq, ckv, qlat, mask, bias, wuvbd)


def _fox_proj_kernel(tiles_per_seq, x_ref, wq_ref, wk_ref, wv_ref, wft_ref, bf_ref,
                     q_ref, k_ref, v_ref, cum_ref, carry_scr):
    i = pl.program_id(0)
    tn = x_ref.shape[0]

    @pl.when(i % tiles_per_seq == 0)
    def _():
        carry_scr[...] = jnp.zeros(carry_scr.shape, F32)

    xb = x_ref[...].astype(BF16)
    q_ref[...] = (_dot(xb, wq_ref[...]) * (B_HEAD_DIM ** -0.5 * LOG2E)).astype(BF16)
    k_ref[...] = _dot(xb, wk_ref[...]).astype(BF16)
    v_ref[...] = _dot(xb, wv_ref[...]).astype(BF16)
    z = _dot_t(wft_ref[...], xb) + bf_ref[...]
    logf = jnp.minimum(z, 0.0) - jnp.log(1.0 + jnp.exp(-jnp.abs(z)))
    r = lax.broadcasted_iota(I32, (tn, tn), 0)
    c = lax.broadcasted_iota(I32, (tn, tn), 1)
    tri = jnp.where(r <= c, 1.0, 0.0).astype(BF16)
    hi = logf.astype(BF16)
    lo = (logf - hi.astype(F32)).astype(BF16)
    cum = _dot(hi, tri) + _dot(lo, tri) + carry_scr[:, 0:1]
    cum_ref[...] = cum * LOG2E
    carry_scr[...] = jnp.broadcast_to(cum[:, tn - 1:tn], carry_scr.shape)


def _fox_proj(x2, seq, wq, wk, wv, wft, bf):
    n, d = x2.shape
    tn = min(TOK_TILE, seq)
    hd = B_HEADS * B_HEAD_DIM
    full = lambda a: pl.BlockSpec(a.shape, lambda i, _nd=a.ndim: (0,) * _nd)
    kernel = functools.partial(_fox_proj_kernel, seq // tn)
    return pl.pallas_call(
        kernel,
        grid=(n // tn,),
        in_specs=[pl.BlockSpec((tn, d), lambda i: (i, 0))] + [full(a) for a in (wq, wk, wv, wft, bf)],
        out_specs=[pl.BlockSpec((tn, hd), lambda i: (i, 0))] * 3 + [pl.BlockSpec((B_HEADS, tn), lambda i: (0, i))],
        out_shape=[jax.ShapeDtypeStruct((n, hd), BF16)] * 3 + [jax.ShapeDtypeStruct((B_HEADS, n), F32)],
        scratch_shapes=[pltpu.VMEM((B_HEADS, LANES), F32)],
        compiler_params=_cparams(("arbitrary",)),
        name="fox_proj",
    )(x2, wq, wk, wv, wft, bf)


def _fox_attn_kernel(q_ref, k_ref, v_ref, cum_ref, o_ref, m_scr, l_scr, acc_scr):
    T = FOX_T
    i = pl.program_id(2)
    lane = lax.broadcasted_iota(I32, (1, LANES), 1)
    lo_half = lane < B_HEAD_DIM
    q = q_ref[...]
    zero = jnp.zeros_like(q)
    q2 = (jnp.where(lo_half, q, zero), jnp.where(lo_half, zero, q))
    m_scr[...] = jnp.full(m_scr.shape, NEG, F32)
    l_scr[...] = jnp.zeros(l_scr.shape, F32)
    acc_scr[...] = jnp.zeros(acc_scr.shape, F32)

    def attend(k0, causal):
        kc = k_ref[pl.ds(k0, T), :]
        vc = v_ref[pl.ds(k0, T), :]
        cs = cum_ref[0, k0 // T]
        pv = []
        alphas = []
        for h in range(2):
            s = _dot_t(q2[h], kc) - cs[h:h + 1, :]
            if causal:
                r = lax.broadcasted_iota(I32, (T, T), 0)
                c = lax.broadcasted_iota(I32, (T, T), 1)
                s = jnp.where(c <= r, s, NEG)
            m_prev = m_scr[h]
            m_new = jnp.maximum(m_prev, jnp.max(s, axis=1, keepdims=True))
            alpha = jnp.exp2(m_prev - m_new)
            p = jnp.exp2(s - jnp.concatenate([m_new] * (T // LANES), axis=1))
            l_scr[h] = alpha * l_scr[h] + jnp.sum(p, axis=1, keepdims=True)
            m_scr[h] = m_new
            pv.append(_dot(p.astype(BF16), vc))
            alphas.append(alpha)
        acc_scr[...] = jnp.where(lo_half, alphas[0], alphas[1]) * acc_scr[...] + jnp.where(lo_half, pv[0], pv[1])

    def far(c, carry):
        attend(pl.multiple_of(c * T, T), False)
        return carry

    lax.fori_loop(0, i, far, 0)
    attend(pl.multiple_of(i * T, T), True)
    o_ref[...] = (acc_scr[...] / jnp.where(lo_half, l_scr[0], l_scr[1])).astype(BF16)


def _fox_attn(bsz, seq, q, k, v, cum3):
    T = FOX_T
    nq = seq // T
    n = bsz * seq
    pairs = B_HEADS // 2
    return pl.pallas_call(
        _fox_attn_kernel,
        grid=(bsz, pairs, nq),
        in_specs=[
            pl.BlockSpec((T, LANES), lambda b, p, i: (b * nq + i, p)),
            pl.BlockSpec((seq, LANES), lambda b, p, i: (b, p)),
            pl.BlockSpec((seq, LANES), lambda b, p, i: (b, p)),
            pl.BlockSpec((1, nq, 2, T), lambda b, p, i: (p, b, 0, 0)),
        ],
        out_specs=pl.BlockSpec((T, LANES), lambda b, p, i: (b * nq + i, p)),
        out_shape=jax.ShapeDtypeStruct((n, B_HEADS * B_HEAD_DIM), BF16),
        scratch_shapes=[
            pltpu.VMEM((2, T, LANES), F32),
            pltpu.VMEM((2, T, LANES), F32),
            pltpu.VMEM((T, LANES), F32),
        ],
        compiler_params=_cparams(("parallel", "parallel", "arbitrary")),
        name="fox_attn",
    )(q, k, v, cum3)


def _fox_mixer(x2, bsz, seq, w_in, b_f):
    hd = B_HEADS * B_HEAD_DIM
    wq = w_in[:, :hd].astype(BF16)
    wk = w_in[:, hd:2 * hd].astype(BF16)
    wv = w_in[:, 2 * hd:3 * hd].astype(BF16)
    wft = jnp.transpose(w_in[:, 3 * hd:]).astype(BF16)
    bf = b_f.reshape(B_HEADS, 1).astype(F32)
    q, k, v, cum = _fox_proj(x2, seq, wq, wk, wv, wft, bf)
    cum3 = jnp.transpose(cum.reshape(B_HEADS // 2, 2, bsz * seq // FOX_T, FOX_T), (0, 2, 1, 3))
    return _fox_attn(bsz, seq, q, k, v, cum3)


def _oproj_route_kernel(a_ref, x_ref, wo_ref, g_ref, b_ref, wr_ref, br_ref,
                        x1_ref, tope_ref, rank_ref, gate_ref, cnt_ref, carry_scr):
    i = pl.program_id(0)
    tn = x_ref.shape[0]
    ne = wr_ref.shape[0]

    @pl.when(i == 0)
    def _():
        carry_scr[...] = jnp.zeros(carry_scr.shape, F32)

    h = _dot(a_ref[...], wo_ref[...])
    x1 = _layer_norm(DEEPNORM_ALPHA * x_ref[...] + h, g_ref[...], b_ref[...])
    x1_ref[...] = x1
    logits = _dot_t(wr_ref[...], x1.astype(BF16)) + br_ref[...]
    eid = lax.broadcasted_iota(I32, (ne, tn), 0).astype(F32)
    work = logits
    vals, hots = [], []
    for k in range(TOP_K):
        m = jnp.max(work, axis=0, keepdims=True)
        idx = jnp.min(jnp.where(work == m, eid, float(ne)), axis=0, keepdims=True)
        hot = eid == idx
        vals.append(m)
        hots.append(hot)
        tope_ref[k:k + 1, :] = idx.astype(I32)
        work = jnp.where(hot, -jnp.inf, work)
    es = [jnp.exp(v - vals[0]) for v in vals]
    den = es[0] + es[1] + es[2] + es[3]
    for k in range(TOP_K):
        gate_ref[k:k + 1, :] = es[k] / den
    sel = jnp.zeros((ne, tn), F32)
    for hot in hots:
        sel = sel + jnp.where(hot, 1.0, 0.0)
    r = lax.broadcasted_iota(I32, (tn, tn), 0)
    c = lax.broadcasted_iota(I32, (tn, tn), 1)
    before = jnp.where(r < c, 1.0, 0.0).astype(BF16)
    carry = carry_scr[:, 0:1]
    prefix = _dot(sel.astype(BF16), before) + carry
    for k in range(TOP_K):
        rank_ref[k:k + 1, :] = jnp.sum(jnp.where(hots[k], prefix, 0.0), axis=0, keepdims=True).astype(I32)
    total = carry + jnp.sum(sel, axis=1, keepdims=True)
    carry_scr[...] = jnp.broadcast_to(total, carry_scr.shape)
    cnt_ref[...] = carry_scr[...].astype(I32)


def _oproj_route(attn, x2, wo, g, b, wr, br):
    n, d = x2.shape
    tn = min(TOK_TILE, n)
    ne = wr.shape[0]
    full = lambda a: pl.BlockSpec(a.shape, lambda i, _nd=a.ndim: (0,) * _nd)
    tok = lambda w: pl.BlockSpec((tn, w), lambda i: (i, 0))
    per_k = pl.BlockSpec((TOP_K, tn), lambda i: (0, i))
    return pl.pallas_call(
        _oproj_route_kernel,
        grid=(n // tn,),
        in_specs=[tok(attn.shape[1]), tok(d)] + [full(a) for a in (wo, g, b, wr, br)],
        out_specs=[tok(d), per_k, per_k, per_k, pl.BlockSpec((ne, LANES), lambda i: (0, 0))],
        out_shape=[
            jax.ShapeDtypeStruct((n, d), F32),
            jax.ShapeDtypeStruct((TOP_K, n), I32),
            jax.ShapeDtypeStruct((TOP_K, n), I32),
            jax.ShapeDtypeStruct((TOP_K, n), F32),
            jax.ShapeDtypeStruct((ne, LANES), I32),
        ],
        scratch_shapes=[pltpu.VMEM((ne, LANES), F32)],
        compiler_params=_cparams(("arbitrary",)),
        name="oproj_route",
    )(attn, x2, wo, g, b, wr, br)


def _dispatch_kernel(zlo_ref, zhi_ref, tv_ref, dest_ref, x_ref, xs_hbm, zero_scr, sem, zsem):
    tn = x_ref.shape[0]
    tm = zero_scr.shape[0]

    def issue(t, carry):
        for k in range(TOP_K):
            d = dest_ref[t * TOP_K + k]
            pltpu.make_async_copy(x_ref.at[pl.ds(t, 1)], xs_hbm.at[pl.ds(d, 1)], sem).start()
        return carry

    lax.fori_loop(0, tn, issue, 0)

    @pl.when(pl.program_id(0) == 0)
    def _():
        zero_scr[...] = jnp.zeros(zero_scr.shape, zero_scr.dtype)
        ne = zlo_ref.shape[0]
        n_tiles = tv_ref.shape[0]

        def pad_row_copy(row):
            return pltpu.make_async_copy(zero_scr.at[pl.ds(0, 1)], xs_hbm.at[pl.ds(row, 1)], zsem)

        def tile_copy(t):
            return pltpu.make_async_copy(zero_scr, xs_hbm.at[pl.ds(pl.multiple_of(t * tm, tm), tm)], zsem)

        def start_pad(e, carry):
            def body(r, c):
                pad_row_copy(zlo_ref[e] + r).start()
                return c

            return lax.fori_loop(0, zhi_ref[e] - zlo_ref[e], body, carry)

        def start_tile(t, carry):
            @pl.when(tv_ref[t] == 0)
            def _():
                tile_copy(t).start()

            return carry

        def wait_pad(e, carry):
            def body(r, c):
                pad_row_copy(0).wait()
                return c

            return lax.fori_loop(0, zhi_ref[e] - zlo_ref[e], body, carry)

        def wait_tile(t, carry):
            @pl.when(tv_ref[t] == 0)
            def _():
                tile_copy(t).wait()

            return carry

        lax.fori_loop(0, ne, start_pad, 0)
        lax.fori_loop(0, n_tiles, start_tile, 0)
        lax.fori_loop(0, ne, wait_pad, 0)
        lax.fori_loop(0, n_tiles, wait_tile, 0)

    for k in range(TOP_K):
        pltpu.make_async_copy(x_ref, xs_hbm.at[pl.ds(0, tn)], sem).wait()


def _dispatch(zlo, zhi, tile_v, dest_flat, x1, n_slots):
    n, d = x1.shape
    tn = min(ROW_TILE, n)
    grid_spec = pltpu.PrefetchScalarGridSpec(
        num_scalar_prefetch=3,
        grid=(n // tn,),
        in_specs=[
            pl.BlockSpec((tn * TOP_K,), lambda i, *_: (i,), memory_space=pltpu.SMEM),
            pl.BlockSpec((tn, d), lambda i, *_: (i, 0)),
        ],
        out_specs=pl.BlockSpec(memory_space=pl.ANY),
        scratch_shapes=[pltpu.VMEM((MOE_TM, d), x1.dtype), pltpu.SemaphoreType.DMA(()), pltpu.SemaphoreType.DMA(())],
    )
    return pl.pallas_call(
        _dispatch_kernel,
        grid_spec=grid_spec,
        out_shape=jax.ShapeDtypeStruct((n_slots, d), x1.dtype),
        compiler_params=pltpu.CompilerParams(dimension_semantics=("arbitrary",), has_side_effects=True),
        name="moe_dispatch",
    )(zlo, zhi, tile_v, dest_flat, x1)


def _expert_kernel(te_ref, tv_ref, xs_ref, wgu_ref, bgu_ref, wd_ref, bd_ref, y_ref, wgu_b, wd_b):
    t = pl.program_id(0)
    f = wd_ref.shape[1]

    @pl.when((t == 0) | (te_ref[t] != te_ref[jnp.maximum(t - 1, 0)]))
    def _():
        wgu_b[...] = wgu_ref[0].astype(BF16)
        wd_b[...] = wd_ref[0].astype(BF16)

    @pl.when(tv_ref[t] > 0)
    def _():
        xb = xs_ref[...].astype(BF16)
        h = _dot(xb, wgu_b[...]) + bgu_ref[0]
        gate = jnp.minimum(h[:, :f], SWIGLU_LIMIT)
        up = jnp.clip(h[:, f:], -SWIGLU_LIMIT, SWIGLU_LIMIT)
        act = (up + 1.0) * (gate * (1.0 / (1.0 + jnp.exp(-SWIGLU_ALPHA * gate))))
        y_ref[...] = _dot(act.astype(BF16), wd_b[...]) + bd_ref[0]

    @pl.when(tv_ref[t] == 0)
    def _():
        y_ref[...] = jnp.zeros(y_ref.shape, F32)


def _experts(layer, tile_e, tile_v, xs, wgu, bgu, wd, bd):
    n_slots, d = xs.shape
    tm = MOE_TM
    f2 = wgu.shape[3]
    f = wd.shape[2]
    grid_spec = pltpu.PrefetchScalarGridSpec(
        num_scalar_prefetch=2,
        grid=(n_slots // tm,),
        in_specs=[
            pl.BlockSpec((tm, d), lambda t, te, tv: (t, 0)),
            pl.BlockSpec((None, 1, d, f2), lambda t, te, tv: (layer, te[t], 0, 0)),
            pl.BlockSpec((1, 1, f2), lambda t, te, tv: (te[t], 0, 0)),
            pl.BlockSpec((None, 1, f, d), lambda t, te, tv: (layer, te[t], 0, 0)),
            pl.BlockSpec((1, 1, d), lambda t, te, tv: (te[t], 0, 0)),
        ],
        out_specs=pl.BlockSpec((tm, d), lambda t, te, tv: (t, 0)),
        scratch_shapes=[pltpu.VMEM((d, f2), BF16), pltpu.VMEM((f, d), BF16)],
    )
    return pl.pallas_call(
        _expert_kernel,
        grid_spec=grid_spec,
        out_shape=jax.ShapeDtypeStruct((n_slots, d), F32),
        compiler_params=_cparams(("arbitrary",)),
        name="moe_experts",
    )(tile_e, tile_v, xs, wgu, bgu, wd, bd)


def _combine_kernel(dest_ref, dest_next_ref, y_hbm, gate_ref, x_ref, g_ref, b_ref, o_ref, ybuf, sem):
    tn = x_ref.shape[0]
    i = pl.program_id(0)
    slot = i % 2

    def issue(dref, s):
        def body(t, carry):
            for k in range(TOP_K):
                d = dref[t * TOP_K + k]
                pltpu.make_async_copy(y_hbm.at[pl.ds(d, 1)], ybuf.at[s, k, pl.ds(t, 1)], sem.at[s]).start()
            return carry

        lax.fori_loop(0, tn, body, 0)

    @pl.when(i == 0)
    def _():
        issue(dest_ref, 0)

    @pl.when(i + 1 < pl.num_programs(0))
    def _():
        issue(dest_next_ref, 1 - slot)

    for k in range(TOP_K):
        pltpu.make_async_copy(y_hbm.at[pl.ds(0, tn)], ybuf.at[slot, k], sem.at[slot]).wait()
    gates = gate_ref[...]
    fsum = gates[:, 0:1] * ybuf[slot, 0]
    for k in range(1, TOP_K):
        fsum = fsum + gates[:, k:k + 1] * ybuf[slot, k]
    o_ref[...] = _layer_norm(DEEPNORM_ALPHA * x_ref[...] + fsum, g_ref[...], b_ref[...])


def _combine(dest_flat, y, gates, x1, g, b):
    n, d = x1.shape
    tn = min(ROW_TILE, n)
    full = lambda a: pl.BlockSpec(a.shape, lambda i, _nd=a.ndim: (0,) * _nd)
    last = n // tn - 1
    return pl.pallas_call(
        _combine_kernel,
        grid=(n // tn,),
        in_specs=[
            pl.BlockSpec((tn * TOP_K,), lambda i: (i,), memory_space=pltpu.SMEM),
            pl.BlockSpec((tn * TOP_K,), lambda i: (jnp.minimum(i + 1, last),), memory_space=pltpu.SMEM),
            pl.BlockSpec(memory_space=pl.ANY),
            pl.BlockSpec((tn, TOP_K), lambda i: (i, 0)),
            pl.BlockSpec((tn, d), lambda i: (i, 0)),
            full(g), full(b),
        ],
        out_specs=pl.BlockSpec((tn, d), lambda i: (i, 0)),
        out_shape=jax.ShapeDtypeStruct((n, d), F32),
        scratch_shapes=[pltpu.VMEM((2, TOP_K, tn, d), F32), pltpu.SemaphoreType.DMA((2,))],
        compiler_params=_cparams(("arbitrary",)),
        name="moe_combine",
    )(dest_flat, dest_flat, y, gates, x1, g, b)


def _moe(layer, x1, top_e, rank, gates, counts, w_gu, b_gu, w_down, b_down, ln_g, ln_b):
    n, d = x1.shape
    ne = w_gu.shape[1]
    tm = MOE_TM
    n_pairs = n * TOP_K
    n_slots = -(-n_pairs // tm) * tm + ne * tm
    n_tiles = n_slots // tm
    counts = counts[:, 0]
    padded = (counts + tm - 1) // tm * tm
    eidx = jnp.arange(ne, dtype=I32)
    pad_ends = jnp.sum(jnp.where(eidx[None, :] <= eidx[:, None], padded[None, :], 0), axis=1)
    pad_starts = pad_ends - padded
    start_of_pair = jnp.sum(jnp.where(top_e[..., None] == eidx, pad_starts, 0), axis=-1)
    dest = jnp.transpose(start_of_pair + rank).reshape(n_pairs).astype(I32)
    gates = jnp.transpose(gates)
    tile_start = jnp.arange(n_tiles, dtype=I32) * tm
    tile_v = (tile_start < pad_ends[ne - 1]).astype(I32)
    tile_e = jnp.minimum(jnp.sum((pad_ends[None, :] <= tile_start[:, None]).astype(I32), axis=1), ne - 1)
    xs = _dispatch((pad_starts + counts).astype(I32), pad_ends.astype(I32), tile_v, dest, x1, n_slots)
    y = _experts(layer, tile_e, tile_v, xs, w_gu, b_gu.reshape(ne, 1, -1).astype(F32),
                 w_down, b_down.reshape(ne, 1, -1).astype(F32))
    return _combine(dest, y, gates, x1, ln_g, ln_b)


def kernel(x, rel_table, a_w_in, a_g_q, a_g_kv, a_g_kidx, a_b_kidx, a_w_uq, a_w_uk, a_w_uv, a_w_qidx, a_w_o,
           b_w_in, b_b_f, b_w_o, ln_mix_g, ln_mix_b, ln_ffn_g, ln_ffn_b, w_router, b_router, w_gu, b_gu,
           w_down, b_down):
    bsz, seq, d = x.shape
    n_mixers = 2
    depth = ln_mix_g.shape[0]
    x2 = x.reshape(bsz * seq, d)
    row = lambda v: v.reshape(1, -1).astype(F32)
    for i in range(depth):
        j = i // n_mixers
        if i % n_mixers == 0:
            attn = _dsa_mixer(x2, bsz, seq, a_w_in[j], a_g_q[j], a_g_kv[j], a_g_kidx[j], a_b_kidx[j],
                              a_w_uq[j], a_w_uk[j], a_w_uv[j], a_w_qidx[j], rel_table)
            wo = a_w_o[j]
        else:
            attn = _fox_mixer(x2, bsz, seq, b_w_in[j], b_b_f[j])
            wo = b_w_o[j]
        x1, top_e, rank, gates, counts = _oproj_route(attn, x2, wo.astype(BF16), row(ln_mix_g[i]), row(ln_mix_b[i]),
                                                      jnp.transpose(w_router[i]).astype(BF16),
                                                      b_router[i].reshape(-1, 1).astype(F32))
        x2 = _moe(i, x1, top_e, rank, gates, counts, w_gu, b_gu[i], w_down, b_down[i],
                  row(ln_ffn_g[i]), row(ln_ffn_b[i]))
    return x2.reshape(bsz, seq, d)
```

```python
import functools
import math

import numpy as np
import jax
import jax.numpy as jnp
from jax import lax
from jax.experimental import pallas as pl
from jax.experimental.pallas import tpu as pltpu

F32 = jnp.float32
BF16 = jnp.bfloat16
I32 = jnp.int32

A_HEADS = 16
A_HEAD_DIM = 64
A_V_DIM = 64
A_Q_RANK = 256
A_KV_RANK = 128
IDX_HEADS = 8
IDX_DIM = 64
IDX_TOPK_MAX = 256
IDX_TOPK_FRAC = 4
B_HEADS = 16
B_HEAD_DIM = 64
REL_BUCKETS = 32
REL_MAX_DIST = 128
TOP_K = 4
SWIGLU_ALPHA = 1.702
SWIGLU_LIMIT = 7.0
LN_EPS = 1e-5
RMS_EPS = 1e-6
DEPTH = 2
DEEPNORM_ALPHA = (2.0 * DEPTH) ** 0.25

LANES = 128
NEG = -1e30
INT_MIN = -2147483648
MASK_KEY = INT_MIN
LOG2E = math.log2(math.e)
VMEM_LIMIT = 56 * 1024 * 1024

TOK_TILE = 512
DSA_T = 128
DSA_KC = 256
DSA_KF = 512
FOX_T = 512
MOE_TM = 512
ROW_TILE = 256


def _dot(a, b):
    return jnp.dot(a, b, preferred_element_type=F32)


def _dot_t(a, b):
    return lax.dot_general(a, b, (((1,), (1,)), ((), ())), preferred_element_type=F32)


def _cparams(sem):
    return pltpu.CompilerParams(dimension_semantics=sem, vmem_limit_bytes=VMEM_LIMIT)


def _layer_norm(v, g, b):
    mu = jnp.mean(v, axis=-1, keepdims=True)
    c = v - mu
    var = jnp.mean(c * c, axis=-1, keepdims=True)
    return c * lax.rsqrt(var + LN_EPS) * g + b


def _rms_norm(v, g):
    return v * lax.rsqrt(jnp.mean(v * v, axis=-1, keepdims=True) + RMS_EPS) * g


def _dsa_proj_kernel(x_ref, wcq_ref, wckv_ref, wki_ref, wwit_ref, gq_ref, gkv_ref, gki_ref, bki_ref,
                     wuq_ref, wuk_ref, wqi_ref,
                     ckv_ref, kidx_ref, widxt_ref, qlat_ref, qidx_ref):
    xb = x_ref[...].astype(BF16)
    cq = _rms_norm(_dot(xb, wcq_ref[...]), gq_ref[...])
    ckv = _rms_norm(_dot(xb, wckv_ref[...]), gkv_ref[...])
    ckv_ref[...] = ckv.astype(BF16)
    kidx = _layer_norm(_dot(xb, wki_ref[...]), gki_ref[...], bki_ref[...])
    kidx_ref[...] = kidx.astype(BF16)
    widxt_ref[...] = _dot_t(wwit_ref[...], xb) * (IDX_HEADS ** -0.5 * IDX_DIM ** -0.5)
    cqb = cq.astype(BF16)
    qb = _dot(cqb, wuq_ref[...]).astype(BF16)
    scale = A_HEAD_DIM ** -0.5 * LOG2E
    for p in range(A_HEADS // 2):
        ql = _dot(qb[:, p * LANES:(p + 1) * LANES], wuk_ref[p]) * scale
        qlat_ref[2 * p] = ql[:, :A_KV_RANK].astype(BF16)
        qlat_ref[2 * p + 1] = ql[:, A_KV_RANK:].astype(BF16)
    qit = _dot_t(wqi_ref[...], cqb).astype(BF16)
    for h in range(IDX_HEADS):
        qidx_ref[h] = qit[h * IDX_DIM:(h + 1) * IDX_DIM, :]


def _dsa_proj(x2, wcq, wckv, wki, wwit, gq, gkv, gki, bki, wuq, wukbd, wqi):
    n, d = x2.shape
    tn = min(TOK_TILE, n)
    full = lambda a: pl.BlockSpec(a.shape, lambda i, _nd=a.ndim: (0,) * _nd)
    weights = (wcq, wckv, wki, wwit, gq, gkv, gki, bki, wuq, wukbd, wqi)
    return pl.pallas_call(
        _dsa_proj_kernel,
        grid=(n // tn,),
        in_specs=[pl.BlockSpec((tn, d), lambda i: (i, 0))] + [full(a) for a in weights],
        out_specs=[
            pl.BlockSpec((tn, A_KV_RANK), lambda i: (i, 0)),
            pl.BlockSpec((tn, IDX_DIM), lambda i: (i, 0)),
            pl.BlockSpec((IDX_HEADS, tn), lambda i: (0, i)),
            pl.BlockSpec((A_HEADS, tn, A_KV_RANK), lambda i: (0, i, 0)),
            pl.BlockSpec((IDX_HEADS, IDX_DIM, tn), lambda i: (0, 0, i)),
        ],
        out_shape=[
            jax.ShapeDtypeStruct((n, A_KV_RANK), BF16),
            jax.ShapeDtypeStruct((n, IDX_DIM), BF16),
            jax.ShapeDtypeStruct((IDX_HEADS, n), F32),
            jax.ShapeDtypeStruct((A_HEADS, n, A_KV_RANK), BF16),
            jax.ShapeDtypeStruct((IDX_HEADS, IDX_DIM, n), BF16),
        ],
        compiler_params=_cparams(("parallel",)),
        name="dsa_proj",
    )(x2, *weights)


def _dsa_select_kernel(topk, qidx_ref, widxt_ref, kidx_ref, mask_ref, keys_scr, hi_scr, lo_scr, jstar_scr):
    T, KC = DSA_T, DSA_KC
    SUB = 8
    i = pl.program_id(1)
    t0 = i * T
    n_blk = i + 1
    n_sc = (n_blk * T + KC - 1) // KC
    t_row = t0 + lax.broadcasted_iota(I32, (1, T), 1)
    wt = widxt_ref[...]
    qit = jnp.concatenate([qidx_ref[h] for h in range(IDX_HEADS)], axis=1)

    def score_chunk(c, carry):
        k0 = pl.multiple_of(c * KC, KC)
        s_all = _dot(kidx_ref[pl.ds(k0, KC), :], qit)
        acc = jnp.zeros((KC, T), F32)
        for h in range(IDX_HEADS):
            acc = acc + wt[h:h + 1, :] * jnp.maximum(s_all[:, h * T:(h + 1) * T], 0.0)
        acc = jnp.where(acc == 0.0, 0.0, acc)
        bits = pltpu.bitcast(acc, I32)
        key = bits ^ ((bits >> 31) & 0x7FFFFFFF)
        k_abs = k0 + lax.broadcasted_iota(I32, (KC, 1), 0)
        key = jnp.where(k_abs <= t_row, key, MASK_KEY)
        for g in range(KC // LANES):
            blk = key[g * LANES:(g + 1) * LANES, :]
            keys_scr[c * (KC // LANES) + g] = blk
            hi_scr[c * (KC // LANES) + g] = (blk >> 16).astype(jnp.int16)
        return carry

    lax.fori_loop(0, n_sc, score_chunk, 0)

    UNR = min(8, keys_scr.shape[0])
    n_scored = n_sc * (KC // LANES)
    n_it = (n_scored + UNR - 1) // UNR

    def mask_fill(j, carry):
        keys_scr[j] = jnp.full((LANES, T), MASK_KEY, I32)
        hi_scr[j] = jnp.full((LANES, T), -32768, jnp.int16)
        return carry

    lax.fori_loop(n_scored, n_it * UNR, mask_fill, 0)

    def fold(hit):
        return jnp.sum(hit.reshape(LANES // SUB, SUB, T), axis=0)

    PACK = 2 * SUB
    one16, zero16 = jnp.int16(1), jnp.int16(0)

    def fold16(hit):
        parts = [hit[r * PACK:(r + 1) * PACK, :] for r in range(LANES // PACK)]
        while len(parts) > 1:
            parts = [a + b for a, b in zip(parts[0::2], parts[1::2])]
        return parts[0]

    def search16(half_scr, base):
        def count_ge(cand_s):
            def body(j, part):
                for g in range(UNR):
                    part = part + fold16(jnp.where(half_scr[UNR * j + g] >= cand_s, one16, zero16))
                return part

            part = lax.fori_loop(0, n_it, body, jnp.zeros((PACK, T), jnp.int16))
            return jnp.sum(part.astype(F32), axis=0, keepdims=True)

        def bit_body(bi, ans):
            cand = ans | jnp.left_shift(jnp.int32(1), 15 - bi)
            cnt = base + count_ge((cand ^ 0x8000).astype(jnp.int16))
            return jnp.where(cnt >= topk, cand, ans)

        return lax.fori_loop(0, 16, bit_body, jnp.zeros((1, T), I32))

    ans_hi = search16(hi_scr, 0.0)
    thr_hi = (ans_hi ^ 0x8000).astype(jnp.int16)

    def low_halves(c, part):
        for g in range(UNR):
            j = UNR * c + g
            hi = hi_scr[j]
            lo = ((keys_scr[j] & 0xFFFF) ^ 0x8000).astype(jnp.int16)
            lo_scr[j] = jnp.where(hi == thr_hi, lo, jnp.int16(-32768))
            part = part + fold16(jnp.where(hi > thr_hi, one16, zero16))
        return part

    part_hi = lax.fori_loop(0, n_it, low_halves, jnp.zeros((PACK, T), jnp.int16))
    cnt_hi = jnp.sum(part_hi.astype(F32), axis=0, keepdims=True)
    ans_lo = search16(lo_scr, cnt_hi)
    thr = jnp.left_shift(ans_hi ^ 0x8000, 16) | ans_lo

    def count_gt_eq():
        def body(j, parts):
            kk = keys_scr[j]
            return (parts[0] + fold(jnp.where(kk > thr, 1.0, 0.0)), parts[1] + fold(jnp.where(kk == thr, 1.0, 0.0)))

        z = jnp.zeros((SUB, T), F32)
        pg, pe = lax.fori_loop(0, n_blk, body, (z, z))
        return jnp.sum(pg, axis=0, keepdims=True), jnp.sum(pe, axis=0, keepdims=True)

    cnt_gt, cnt_eq = count_gt_eq()
    need = topk - cnt_gt
    s_total = keys_scr.shape[0] * LANES
    jstar_scr[...] = jnp.full(jstar_scr.shape, s_total, I32)
    tie = jnp.max(jnp.where(cnt_gt + cnt_eq > topk, 1.0, 0.0)) > 0.0

    @pl.when(tie)
    def _():
        n_bits = max(1, (s_total - 1).bit_length())

        def jbit(bi, lo):
            cand = lo + jnp.left_shift(jnp.int32(1), n_bits - 1 - bi)

            def body(j, part):
                pos = j * LANES + lax.broadcasted_iota(I32, (LANES, T), 0)
                hit = jnp.where(keys_scr[j] == thr, 1.0, 0.0)
                return part + fold(jnp.where(pos < cand, hit, 0.0))

            part = lax.fori_loop(0, n_blk, body, jnp.zeros((SUB, T), F32))
            c = jnp.sum(part, axis=0, keepdims=True)
            return jnp.where(c < need, cand, lo)

        lo = lax.fori_loop(0, n_bits, jbit, jnp.zeros((1, T), I32))
        jstar_scr[...] = jnp.broadcast_to(lo, jstar_scr.shape)

    jstar = jstar_scr[0:1, :]

    def mask_chunk(c, carry):
        for g in range(KC // LANES):
            j = c * (KC // LANES) + g
            kk = keys_scr[j]
            k_abs = j * LANES + lax.broadcasted_iota(I32, (LANES, 1), 0)
            at_thr = jnp.where(kk == thr, jnp.where(k_abs <= jstar, 0.0, NEG), NEG)
            madd = jnp.where(kk > thr, 0.0, at_thr)
            mask_ref[0, j] = jnp.where(k_abs <= t_row, madd, NEG).T.astype(BF16)
        return carry

    lax.fori_loop(0, n_sc, mask_chunk, 0)

    def mask_rest(j, carry):
        mask_ref[0, j] = jnp.full((T, LANES), NEG, BF16)
        return carry

    lax.fori_loop(n_sc * (KC // LANES), mask_ref.shape[1], mask_rest, 0)


def _dsa_select(bsz, seq, qidx, widx, kidx):
    T = DSA_T
    nq = seq // T
    topk = min(IDX_TOPK_MAX, seq // IDX_TOPK_FRAC)
    kernel = functools.partial(_dsa_select_kernel, float(topk))
    return pl.pallas_call(
        kernel,
        grid=(bsz, nq),
        in_specs=[
            pl.BlockSpec((IDX_HEADS, IDX_DIM, T), lambda b, i: (0, 0, b * nq + i)),
            pl.BlockSpec((IDX_HEADS, T), lambda b, i: (0, b * nq + i)),
            pl.BlockSpec((seq, IDX_DIM), lambda b, i: (b, 0)),
        ],
        out_specs=pl.BlockSpec((1, seq // LANES, T, LANES), lambda b, i: (b * nq + i, 0, 0, 0)),
        out_shape=jax.ShapeDtypeStruct((bsz * nq, seq // LANES, T, LANES), BF16),
        scratch_shapes=[
            pltpu.VMEM((seq // LANES, LANES, T), I32),
            pltpu.VMEM((seq // LANES, LANES, T), jnp.int16),
            pltpu.VMEM((seq // LANES, LANES, T), jnp.int16),
            pltpu.VMEM((8, T), I32),
        ],
        compiler_params=_cparams(("parallel", "arbitrary")),
        name="dsa_select",
    )(qidx, widx, kidx)


def _dsa_attn_kernel(ckv_ref, qlat_ref, mask_ref, bias_ref, wuv_ref, o_ref, m_scr, l_scr, acc_scr):
    T, KC, KF, H = DSA_T, DSA_KC, DSA_KF, A_HEADS
    t0 = pl.program_id(1) * T
    m_scr[...] = jnp.full(m_scr.shape, NEG, F32)
    l_scr[...] = jnp.zeros(l_scr.shape, F32)
    acc_scr[...] = jnp.zeros(acc_scr.shape, F32)
    q = qlat_ref[...].reshape(H * T, A_KV_RANK)

    def attend(blk0, k0, kw, limit, bias):
        kv = ckv_ref[pl.ds(k0, kw), :]
        s = _dot_t(q, kv)
        madd = jnp.concatenate([mask_ref[0, blk0 + g] for g in range(kw // LANES)], axis=1).astype(F32)
        if limit is not None:
            s_abs = k0 + lax.broadcasted_iota(I32, (1, kw), 1)
            madd = jnp.where(s_abs < limit, madd, NEG)
        s3 = s.reshape(H, T, kw) + madd[None]
        if bias is not None:
            s3 = s3 + bias
        s = s3.reshape(H * T, kw)
        m_prev = m_scr[...]
        m_new = jnp.maximum(m_prev, jnp.max(s, axis=1, keepdims=True))
        alpha = jnp.exp2(m_prev - m_new)
        p = jnp.exp2(s - jnp.concatenate([m_new] * (kw // LANES), axis=1))
        l_scr[...] = alpha * l_scr[...] + jnp.sum(p, axis=1, keepdims=True)
        acc_scr[...] = alpha * acc_scr[...] + _dot(p.astype(BF16), kv)
        m_scr[...] = m_new

    far_end = jnp.maximum(t0 - T, 0)
    n_far = (far_end + KF - 1) // KF

    def far_chunk(c, carry):
        k0 = pl.multiple_of(c * KF, KF)
        attend(c * (KF // LANES), k0, KF, far_end, None)
        return carry

    lax.fori_loop(0, n_far, far_chunk, 0)
    near0 = pl.multiple_of(far_end, LANES)
    attend(far_end // LANES, near0, KC, None, bias_ref[0])

    o = (acc_scr[...] / l_scr[...]).astype(BF16).reshape(H, T, A_KV_RANK)
    for p in range(H // 2):
        pair = jnp.concatenate([o[2 * p], o[2 * p + 1]], axis=1)
        o_ref[:, p * LANES:(p + 1) * LANES] = _dot(pair, wuv_ref[p]).astype(BF16)


def _dsa_attn(bsz, seq, ckv, qlat, mask, bias, wuvbd):
    T = DSA_T
    nq = seq // T
    n = bsz * seq
    return pl.pallas_call(
        _dsa_attn_kernel,
        grid=(bsz, nq),
        in_specs=[
            pl.BlockSpec((seq, A_KV_RANK), lambda b, i: (b, 0)),
            pl.BlockSpec((A_HEADS, T, A_KV_RANK), lambda b, i: (0, b * nq + i, 0)),
            pl.BlockSpec((1, seq // LANES, T, LANES), lambda b, i: (b * nq + i, 0, 0, 0)),
            pl.BlockSpec((1, A_HEADS, T, DSA_KC), lambda b, i: (jnp.minimum(i, 1), 0, 0, 0)),
            pl.BlockSpec(wuvbd.shape, lambda b, i: (0, 0, 0)),
        ],
        out_specs=pl.BlockSpec((T, A_HEADS * A_V_DIM), lambda b, i: (b * nq + i, 0)),
        out_shape=jax.ShapeDtypeStruct((n, A_HEADS * A_V_DIM), BF16),
        scratch_shapes=[
            pltpu.VMEM((A_HEADS * T, LANES), F32),
            pltpu.VMEM((A_HEADS * T, LANES), F32),
            pltpu.VMEM((A_HEADS * T, A_KV_RANK), F32),
        ],
        compiler_params=_cparams(("parallel", "arbitrary")),
        name="dsa_attn",
    )(ckv, qlat, mask, bias, wuvbd)


def _t5_bucket_np(dist):
    max_exact = REL_BUCKETS // 2
    n = np.maximum(dist, 0)
    ratio = np.log(np.maximum(n, 1).astype(np.float32) / np.float32(max_exact)) / np.float32(math.log(REL_MAX_DIST / max_exact))
    large = max_exact + (ratio * np.float32(REL_BUCKETS - max_exact)).astype(np.int32)
    large = np.minimum(large, REL_BUCKETS - 1)
    return np.where(n < max_exact, n, large)


def _dsa_bias_tiles(rel_table):
    T, KC, H = DSA_T, DSA_KC, A_HEADS
    far_bucket = int(_t5_bucket_np(np.array([T + 1]))[0])
    assert far_bucket == REL_BUCKETS - 1 and int(_t5_bucket_np(np.array([T - 15]))[0]) == far_bucket
    L = KC + T - 1
    delta = np.arange(L)
    d_first = (T - 1) - delta
    d_rest = (2 * T - 1) - delta
    idx = np.stack([_t5_bucket_np(d_first), _t5_bucket_np(d_rest)])
    onehot = jnp.asarray((idx[..., None] == np.arange(REL_BUCKETS)).astype(np.float32))
    vec = jnp.einsum("slb,bh->shl", onehot, (rel_table - rel_table[far_bucket]) * LOG2E,
                     precision=lax.Precision.HIGHEST)
    stream = jnp.tile(jnp.pad(vec, ((0, 0), (0, 0), (0, 1))), (1, 1, T))[..., :T * L]
    return stream.reshape(2, H, T, L)[..., T - 1:T - 1 + KC].astype(F32)


def _block_diag_pairs(w):
    h, a, b = w.shape
    z = jnp.zeros((h // 2, a, b), w.dtype)
    top = jnp.concatenate([w[0::2], z], axis=2)
    bot = jnp.concatenate([z, w[1::2]], axis=2)
    return jnp.concatenate([top, bot], axis=1)


def _dsa_mixer(x2, bsz, seq, w_in, g_q, g_kv, g_kidx, b_kidx, w_uq, w_uk, w_uv, w_qidx, rel_table):
    d = x2.shape[1]
    wcq = w_in[:, :A_Q_RANK].astype(BF16)
    wckv = w_in[:, A_Q_RANK:A_Q_RANK + A_KV_RANK].astype(BF16)
    wki = w_in[:, A_Q_RANK + A_KV_RANK:A_Q_RANK + A_KV_RANK + IDX_DIM].astype(BF16)
    wwit = jnp.transpose(w_in[:, A_Q_RANK + A_KV_RANK + IDX_DIM:]).astype(BF16)
    wuq = w_uq.reshape(A_Q_RANK, A_HEADS * A_HEAD_DIM).astype(BF16)
    wukbd = _block_diag_pairs(jnp.transpose(w_uk, (1, 2, 0))).astype(BF16)
    wuvbd = _block_diag_pairs(jnp.transpose(w_uv, (1, 0, 2))).astype(BF16)
    wqi = jnp.transpose(w_qidx.reshape(A_Q_RANK, IDX_HEADS * IDX_DIM)).astype(BF16)
    row = lambda v: v.reshape(1, -1).astype(F32)
    ckv, kidx, widx, qlat, qidx = _dsa_proj(x2, wcq, wckv, wki, wwit, row(g_q), row(g_kv), row(g_kidx), row(b_kidx),
                                            wuq, wukbd, wqi)
    bias = _dsa_bias_tiles(rel_table)
    mask = _dsa_select(bsz, seq, qidx, widx, kidx)
    return _dsa_attn(bsz, seq, ckv, qlat, mask, bias, wuvbd)


def _fox_proj_kernel(tiles_per_seq, x_ref, wq_ref, wk_ref, wv_ref, wft_ref, bf_ref,
                     q_ref, k_ref, v_ref, cum_ref, carry_scr):
    i = pl.program_id(0)
    tn = x_ref.shape[0]

    @pl.when(i % tiles_per_seq == 0)
    def _():
        carry_scr[...] = jnp.zeros(carry_scr.shape, F32)

    xb = x_ref[...].astype(BF16)
    q_ref[...] = (_dot(xb, wq_ref[...]) * (B_HEAD_DIM ** -0.5 * LOG2E)).astype(BF16)
    k_ref[...] = _dot(xb, wk_ref[...]).astype(BF16)
    v_ref[...] = _dot(xb, wv_ref[...]).astype(BF16)
    z = _dot_t(wft_ref[...], xb) + bf_ref[...]
    logf = jnp.minimum(z, 0.0) - jnp.log(1.0 + jnp.exp(-jnp.abs(z)))
    r = lax.broadcasted_iota(I32, (tn, tn), 0)
    c = lax.broadcasted_iota(I32, (tn, tn), 1)
    tri = jnp.where(r <= c, 1.0, 0.0).astype(BF16)
    hi = logf.astype(BF16)
    lo = (logf - hi.astype(F32)).astype(BF16)
    cum = _dot(hi, tri) + _dot(lo, tri) + carry_scr[:, 0:1]
    cum_ref[...] = cum * LOG2E
    carry_scr[...] = jnp.broadcast_to(cum[:, tn - 1:tn], carry_scr.shape)


def _fox_proj(x2, seq, wq, wk, wv, wft, bf):
    n, d = x2.shape
    tn = min(TOK_TILE, seq)
    hd = B_HEADS * B_HEAD_DIM
    full = lambda a: pl.BlockSpec(a.shape, lambda i, _nd=a.ndim: (0,) * _nd)
    kernel = functools.partial(_fox_proj_kernel, seq // tn)
    return pl.pallas_call(
        kernel,
        grid=(n // tn,),
        in_specs=[pl.BlockSpec((tn, d), lambda i: (i, 0))] + [full(a) for a in (wq, wk, wv, wft, bf)],
        out_specs=[pl.BlockSpec((tn, hd), lambda i: (i, 0))] * 3 + [pl.BlockSpec((B_HEADS, tn), lambda i: (0, i))],
        out_shape=[jax.ShapeDtypeStruct((n, hd), BF16)] * 3 + [jax.ShapeDtypeStruct((B_HEADS, n), F32)],
        scratch_shapes=[pltpu.VMEM((B_HEADS, LANES), F32)],
        compiler_params=_cparams(("arbitrary",)),
        name="fox_proj",
    )(x2, wq, wk, wv, wft, bf)


def _fox_attn_kernel(q_ref, k_ref, v_ref, cum_ref, o_ref, m_scr, l_scr, acc_scr):
    T = FOX_T
    i = pl.program_id(2)
    lane = lax.broadcasted_iota(I32, (1, LANES), 1)
    lo_half = lane < B_HEAD_DIM
    q = q_ref[...]
    zero = jnp.zeros_like(q)
    q2 = (jnp.where(lo_half, q, zero), jnp.where(lo_half, zero, q))
    m_scr[...] = jnp.full(m_scr.shape, NEG, F32)
    l_scr[...] = jnp.zeros(l_scr.shape, F32)
    acc_scr[...] = jnp.zeros(acc_scr.shape, F32)

    def attend(k0, causal):
        kc = k_ref[pl.ds(k0, T), :]
        vc = v_ref[pl.ds(k0, T), :]
        cs = cum_ref[0, k0 // T]
        pv = []
        alphas = []
        for h in range(2):
            s = _dot_t(q2[h], kc) - cs[h:h + 1, :]
            if causal:
                r = lax.broadcasted_iota(I32, (T, T), 0)
                c = lax.broadcasted_iota(I32, (T, T), 1)
                s = jnp.where(c <= r, s, NEG)
            m_prev = m_scr[h]
            m_new = jnp.maximum(m_prev, jnp.max(s, axis=1, keepdims=True))
            alpha = jnp.exp2(m_prev - m_new)
            p = jnp.exp2(s - jnp.concatenate([m_new] * (T // LANES), axis=1))
            l_scr[h] = alpha * l_scr[h] + jnp.sum(p, axis=1, keepdims=True)
            m_scr[h] = m_new
            pv.append(_dot(p.astype(BF16), vc))
            alphas.append(alpha)
        acc_scr[...] = jnp.where(lo_half, alphas[0], alphas[1]) * acc_scr[...] + jnp.where(lo_half, pv[0], pv[1])

    def far(c, carry):
        attend(pl.multiple_of(c * T, T), False)
        return carry

    lax.fori_loop(0, i, far, 0)
    attend(pl.multiple_of(i * T, T), True)
    o_ref[...] = (acc_scr[...] / jnp.where(lo_half, l_scr[0], l_scr[1])).astype(BF16)


def _fox_attn(bsz, seq, q, k, v, cum3):
    T = FOX_T
    nq = seq // T
    n = bsz * seq
    pairs = B_HEADS // 2
    return pl.pallas_call(
        _fox_attn_kernel,
        grid=(bsz, pairs, nq),
        in_specs=[
            pl.BlockSpec((T, LANES), lambda b, p, i: (b * nq + i, p)),
            pl.BlockSpec((seq, LANES), lambda b, p, i: (b, p)),
            pl.BlockSpec((seq, LANES), lambda b, p, i: (b, p)),
            pl.BlockSpec((1, nq, 2, T), lambda b, p, i: (p, b, 0, 0)),
        ],
        out_specs=pl.BlockSpec((T, LANES), lambda b, p, i: (b * nq + i, p)),
        out_shape=jax.ShapeDtypeStruct((n, B_HEADS * B_HEAD_DIM), BF16),
        scratch_shapes=[
            pltpu.VMEM((2, T, LANES), F32),
            pltpu.VMEM((2, T, LANES), F32),
            pltpu.VMEM((T, LANES), F32),
        ],
        compiler_params=_cparams(("parallel", "parallel", "arbitrary")),
        name="fox_attn",
    )(q, k, v, cum3)


def _fox_mixer(x2, bsz, seq, w_in, b_f):
    hd = B_HEADS * B_HEAD_DIM
    wq = w_in[:, :hd].astype(BF16)
    wk = w_in[:, hd:2 * hd].astype(BF16)
    wv = w_in[:, 2 * hd:3 * hd].astype(BF16)
    wft = jnp.transpose(w_in[:, 3 * hd:]).astype(BF16)
    bf = b_f.reshape(B_HEADS, 1).astype(F32)
    q, k, v, cum = _fox_proj(x2, seq, wq, wk, wv, wft, bf)
    cum3 = jnp.transpose(cum.reshape(B_HEADS // 2, 2, bsz * seq // FOX_T, FOX_T), (0, 2, 1, 3))
    return _fox_attn(bsz, seq, q, k, v, cum3)


def _oproj_route_kernel(a_ref, x_ref, wo_ref, g_ref, b_ref, wr_ref, br_ref,
                        x1_ref, tope_ref, rank_ref, gate_ref, cnt_ref, carry_scr):
    i = pl.program_id(0)
    tn = x_ref.shape[0]
    ne = wr_ref.shape[0]

    @pl.when(i == 0)
    def _():
        carry_scr[...] = jnp.zeros(carry_scr.shape, F32)

    h = _dot(a_ref[...], wo_ref[...])
    x1 = _layer_norm(DEEPNORM_ALPHA * x_ref[...] + h, g_ref[...], b_ref[...])
    x1_ref[...] = x1
    logits = _dot_t(wr_ref[...], x1.astype(BF16)) + br_ref[...]
    eid = lax.broadcasted_iota(I32, (ne, tn), 0).astype(F32)
    work = logits
    vals, hots = [], []
    for k in range(TOP_K):
        m = jnp.max(work, axis=0, keepdims=True)
        idx = jnp.min(jnp.where(work == m, eid, float(ne)), axis=0, keepdims=True)
        hot = eid == idx
        vals.append(m)
        hots.append(hot)
        tope_ref[k:k + 1, :] = idx.astype(I32)
        work = jnp.where(hot, -jnp.inf, work)
    es = [jnp.exp(v - vals[0]) for v in vals]
    den = es[0] + es[1] + es[2] + es[3]
    for k in range(TOP_K):
        gate_ref[k:k + 1, :] = es[k] / den
    sel = jnp.zeros((ne, tn), F32)
    for hot in hots:
        sel = sel + jnp.where(hot, 1.0, 0.0)
    r = lax.broadcasted_iota(I32, (tn, tn), 0)
    c = lax.broadcasted_iota(I32, (tn, tn), 1)
    before = jnp.where(r < c, 1.0, 0.0).astype(BF16)
    carry = carry_scr[:, 0:1]
    prefix = _dot(sel.astype(BF16), before) + carry
    for k in range(TOP_K):
        rank_ref[k:k + 1, :] = jnp.sum(jnp.where(hots[k], prefix, 0.0), axis=0, keepdims=True).astype(I32)
    total = carry + jnp.sum(sel, axis=1, keepdims=True)
    carry_scr[...] = jnp.broadcast_to(total, carry_scr.shape)
    cnt_ref[...] = carry_scr[...].astype(I32)


def _oproj_route(attn, x2, wo, g, b, wr, br):
    n, d = x2.shape
    tn = min(TOK_TILE, n)
    ne = wr.shape[0]
    full = lambda a: pl.BlockSpec(a.shape, lambda i, _nd=a.ndim: (0,) * _nd)
    tok = lambda w: pl.BlockSpec((tn, w), lambda i: (i, 0))
    per_k = pl.BlockSpec((TOP_K, tn), lambda i: (0, i))
    return pl.pallas_call(
        _oproj_route_kernel,
        grid=(n // tn,),
        in_specs=[tok(attn.shape[1]), tok(d)] + [full(a) for a in (wo, g, b, wr, br)],
        out_specs=[tok(d), per_k, per_k, per_k, pl.BlockSpec((ne, LANES), lambda i: (0, 0))],
        out_shape=[
            jax.ShapeDtypeStruct((n, d), F32),
            jax.ShapeDtypeStruct((TOP_K, n), I32),
            jax.ShapeDtypeStruct((TOP_K, n), I32),
            jax.ShapeDtypeStruct((TOP_K, n), F32),
            jax.ShapeDtypeStruct((ne, LANES), I32),
        ],
        scratch_shapes=[pltpu.VMEM((ne, LANES), F32)],
        compiler_params=_cparams(("arbitrary",)),
        name="oproj_route",
    )(attn, x2, wo, g, b, wr, br)


def _dispatch_kernel(zlo_ref, zhi_ref, tv_ref, dest_ref, x_ref, xs_hbm, zero_scr, sem, zsem):
    tn = x_ref.shape[0]
    tm = zero_scr.shape[0]

    def issue(t, carry):
        for k in range(TOP_K):
            d = dest_ref[t * TOP_K + k]
            pltpu.make_async_copy(x_ref.at[pl.ds(t, 1)], xs_hbm.at[pl.ds(d, 1)], sem).start()
        return carry

    lax.fori_loop(0, tn, issue, 0)

    @pl.when(pl.program_id(0) == 0)
    def _():
        zero_scr[...] = jnp.zeros(zero_scr.shape, zero_scr.dtype)
        ne = zlo_ref.shape[0]
        n_tiles = tv_ref.shape[0]

        def pad_row_copy(row):
            return pltpu.make_async_copy(zero_scr.at[pl.ds(0, 1)], xs_hbm.at[pl.ds(row, 1)], zsem)

        def tile_copy(t):
            return pltpu.make_async_copy(zero_scr, xs_hbm.at[pl.ds(pl.multiple_of(t * tm, tm), tm)], zsem)

        def start_pad(e, carry):
            def body(r, c):
                pad_row_copy(zlo_ref[e] + r).start()
                return c

            return lax.fori_loop(0, zhi_ref[e] - zlo_ref[e], body, carry)

        def start_tile(t, carry):
            @pl.when(tv_ref[t] == 0)
            def _():
                tile_copy(t).start()

            return carry

        def wait_pad(e, carry):
            def body(r, c):
                pad_row_copy(0).wait()
                return c

            return lax.fori_loop(0, zhi_ref[e] - zlo_ref[e], body, carry)

        def wait_tile(t, carry):
            @pl.when(tv_ref[t] == 0)
            def _():
                tile_copy(t).wait()

            return carry

        lax.fori_loop(0, ne, start_pad, 0)
        lax.fori_loop(0, n_tiles, start_tile, 0)
        lax.fori_loop(0, ne, wait_pad, 0)
        lax.fori_loop(0, n_tiles, wait_tile, 0)

    for k in range(TOP_K):
        pltpu.make_async_copy(x_ref, xs_hbm.at[pl.ds(0, tn)], sem).wait()


def _dispatch(zlo, zhi, tile_v, dest_flat, x1, n_slots):
    n, d = x1.shape
    tn = min(ROW_TILE, n)
    grid_spec = pltpu.PrefetchScalarGridSpec(
        num_scalar_prefetch=3,
        grid=(n // tn,),
        in_specs=[
            pl.BlockSpec((tn * TOP_K,), lambda i, *_: (i,), memory_space=pltpu.SMEM),
            pl.BlockSpec((tn, d), lambda i, *_: (i, 0)),
        ],
        out_specs=pl.BlockSpec(memory_space=pl.ANY),
        scratch_shapes=[pltpu.VMEM((MOE_TM, d), x1.dtype), pltpu.SemaphoreType.DMA(()), pltpu.SemaphoreType.DMA(())],
    )
    return pl.pallas_call(
        _dispatch_kernel,
        grid_spec=grid_spec,
        out_shape=jax.ShapeDtypeStruct((n_slots, d), x1.dtype),
        compiler_params=pltpu.CompilerParams(dimension_semantics=("arbitrary",), has_side_effects=True),
        name="moe_dispatch",
    )(zlo, zhi, tile_v, dest_flat, x1)


def _expert_kernel(te_ref, tv_ref, xs_ref, wgu_ref, bgu_ref, wd_ref, bd_ref, y_ref, wgu_b, wd_b):
    t = pl.program_id(0)
    f = wd_ref.shape[1]

    @pl.when((t == 0) | (te_ref[t] != te_ref[jnp.maximum(t - 1, 0)]))
    def _():
        wgu_b[...] = wgu_ref[0].astype(BF16)
        wd_b[...] = wd_ref[0].astype(BF16)

    @pl.when(tv_ref[t] > 0)
    def _():
        xb = xs_ref[...].astype(BF16)
        h = _dot(xb, wgu_b[...]) + bgu_ref[0]
        gate = jnp.minimum(h[:, :f], SWIGLU_LIMIT)
        up = jnp.clip(h[:, f:], -SWIGLU_LIMIT, SWIGLU_LIMIT)
        act = (up + 1.0) * (gate * (1.0 / (1.0 + jnp.exp(-SWIGLU_ALPHA * gate))))
        y_ref[...] = _dot(act.astype(BF16), wd_b[...]) + bd_ref[0]

    @pl.when(tv_ref[t] == 0)
    def _():
        y_ref[...] = jnp.zeros(y_ref.shape, F32)


def _experts(layer, tile_e, tile_v, xs, wgu, bgu, wd, bd):
    n_slots, d = xs.shape
    tm = MOE_TM
    f2 = wgu.shape[3]
    f = wd.shape[2]
    grid_spec = pltpu.PrefetchScalarGridSpec(
        num_scalar_prefetch=2,
        grid=(n_slots // tm,),
        in_specs=[
            pl.BlockSpec((tm, d), lambda t, te, tv: (t, 0)),
            pl.BlockSpec((None, 1, d, f2), lambda t, te, tv: (layer, te[t], 0, 0)),
            pl.BlockSpec((1, 1, f2), lambda t, te, tv: (te[t], 0, 0)),
            pl.BlockSpec((None, 1, f, d), lambda t, te, tv: (layer, te[t], 0, 0)),
            pl.BlockSpec((1, 1, d), lambda t, te, tv: (te[t], 0, 0)),
        ],
        out_specs=pl.BlockSpec((tm, d), lambda t, te, tv: (t, 0)),
        scratch_shapes=[pltpu.VMEM((d, f2), BF16), pltpu.VMEM((f, d), BF16)],
    )
    return pl.pallas_call(
        _expert_kernel,
        grid_spec=grid_spec,
        out_shape=jax.ShapeDtypeStruct((n_slots, d), F32),
        compiler_params=_cparams(("arbitrary",)),
        name="moe_experts",
    )(tile_e, tile_v, xs, wgu, bgu, wd, bd)


def _combine_kernel(dest_ref, dest_next_ref, y_hbm, gate_ref, x_ref, g_ref, b_ref, o_ref, ybuf, sem):
    tn = x_ref.shape[0]
    i = pl.program_id(0)
    slot = i % 2

    def issue(dref, s):
        def body(t, carry):
            for k in range(TOP_K):
                d = dref[t * TOP_K + k]
                pltpu.make_async_copy(y_hbm.at[pl.ds(d, 1)], ybuf.at[s, k, pl.ds(t, 1)], sem.at[s]).start()
            return carry

        lax.fori_loop(0, tn, body, 0)

    @pl.when(i == 0)
    def _():
        issue(dest_ref, 0)

    @pl.when(i + 1 < pl.num_programs(0))
    def _():
        issue(dest_next_ref, 1 - slot)

    for k in range(TOP_K):
        pltpu.make_async_copy(y_hbm.at[pl.ds(0, tn)], ybuf.at[slot, k], sem.at[slot]).wait()
    gates = gate_ref[...]
    fsum = gates[:, 0:1] * ybuf[slot, 0]
    for k in range(1, TOP_K):
        fsum = fsum + gates[:, k:k + 1] * ybuf[slot, k]
    o_ref[...] = _layer_norm(DEEPNORM_ALPHA * x_ref[...] + fsum, g_ref[...], b_ref[...])


def _combine(dest_flat, y, gates, x1, g, b):
    n, d = x1.shape
    tn = min(ROW_TILE, n)
    full = lambda a: pl.BlockSpec(a.shape, lambda i, _nd=a.ndim: (0,) * _nd)
    last = n // tn - 1
    return pl.pallas_call(
        _combine_kernel,
        grid=(n // tn,),
        in_specs=[
            pl.BlockSpec((tn * TOP_K,), lambda i: (i,), memory_space=pltpu.SMEM),
            pl.BlockSpec((tn * TOP_K,), lambda i: (jnp.minimum(i + 1, last),), memory_space=pltpu.SMEM),
            pl.BlockSpec(memory_space=pl.ANY),
            pl.BlockSpec((tn, TOP_K), lambda i: (i, 0)),
            pl.BlockSpec((tn, d), lambda i: (i, 0)),
            full(g), full(b),
        ],
        out_specs=pl.BlockSpec((tn, d), lambda i: (i, 0)),
        out_shape=jax.ShapeDtypeStruct((n, d), F32),
        scratch_shapes=[pltpu.VMEM((2, TOP_K, tn, d), F32), pltpu.SemaphoreType.DMA((2,))],
        compiler_params=_cparams(("arbitrary",)),
        name="moe_combine",
    )(dest_flat, dest_flat, y, gates, x1, g, b)


def _moe(layer, x1, top_e, rank, gates, counts, w_gu, b_gu, w_down, b_down, ln_g, ln_b):
    n, d = x1.shape
    ne = w_gu.shape[1]
    tm = MOE_TM
    n_pairs = n * TOP_K
    n_slots = -(-n_pairs // tm) * tm + ne * tm
    n_tiles = n_slots // tm
    counts = counts[:, 0]
    padded = (counts + tm - 1) // tm * tm
    eidx = jnp.arange(ne, dtype=I32)
    pad_ends = jnp.sum(jnp.where(eidx[None, :] <= eidx[:, None], padded[None, :], 0), axis=1)
    pad_starts = pad_ends - padded
    start_of_pair = jnp.sum(jnp.where(top_e[..., None] == eidx, pad_starts, 0), axis=-1)
    dest = jnp.transpose(start_of_pair + rank).reshape(n_pairs).astype(I32)
    gates = jnp.transpose(gates)
    tile_start = jnp.arange(n_tiles, dtype=I32) * tm
    tile_v = (tile_start < pad_ends[ne - 1]).astype(I32)
    tile_e = jnp.minimum(jnp.sum((pad_ends[None, :] <= tile_start[:, None]).astype(I32), axis=1), ne - 1)
    xs = _dispatch((pad_starts + counts).astype(I32), pad_ends.astype(I32), tile_v, dest, x1, n_slots)
    y = _experts(layer, tile_e, tile_v, xs, w_gu, b_gu.reshape(ne, 1, -1).astype(F32),
                 w_down, b_down.reshape(ne, 1, -1).astype(F32))
    return _combine(dest, y, gates, x1, ln_g, ln_b)


def kernel(x, rel_table, a_w_in, a_g_q, a_g_kv, a_g_kidx, a_b_kidx, a_w_uq, a_w_uk, a_w_uv, a_w_qidx, a_w_o,
           b_w_in, b_b_f, b_w_o, ln_mix_g, ln_mix_b, ln_ffn_g, ln_ffn_b, w_router, b_router, w_gu, b_gu,
           w_down, b_down):
    bsz, seq, d = x.shape
    n_mixers = 2
    depth = ln_mix_g.shape[0]
    x2 = x.reshape(bsz * seq, d)
    row = lambda v: v.reshape(1, -1).astype(F32)
    for i in range(depth):
        j = i // n_mixers
        if i % n_mixers == 0:
            attn = _dsa_mixer(x2, bsz, seq, a_w_in[j], a_g_q[j], a_g_kv[j], a_g_kidx[j], a_b_kidx[j],
                              a_w_uq[j], a_w_uk[j], a_w_uv[j], a_w_qidx[j], rel_table)
            wo = a_w_o[j]
        else:
            attn = _fox_mixer(x2, bsz, seq, b_w_in[j], b_b_f[j])
            wo = b_w_o[j]
        x1, top_e, rank, gates, counts = _oproj_route(attn, x2, wo.astype(BF16), row(ln_mix_g[i]), row(ln_mix_b[i]),
                                                      jnp.transpose(w_router[i]).astype(BF16),
                                                      b_router[i].reshape(-1, 1).astype(F32))
        x2 = _moe(i, x1, top_e, rank, gates, counts, w_gu, b_gu[i], w_down, b_down[i],
                  row(ln_ffn_g[i]), row(ln_ffn_b[i]))
    return x2.reshape(bsz, seq, d)
```
